```python
import jax, jax.numpy as jnp
from jax import lax
import numpy as np

D_MODEL = 2048
BATCH = 4
SEQ = 4096
DEPTH = 2

GRID_W = 64
CTX_LEN = 256
HEAD_DIM = 128
A_HEADS = 6
A_KV = 2
B_HEADS = 4
C_HEADS = 6
C_KV = 2
WINDOW = 128
BLOCK = 128
NA_H = 8
NA_W = 16
ROPE_THETA = 10000.0
D_FF = 5632
CONV_W = 3
N_BRANCH = 3
EPS = 1e-6
NEG = -1e30
A_WIDTH = A_HEADS * HEAD_DIM
B_WIDTH = B_HEADS * HEAD_DIM
C_WIDTH = C_HEADS * HEAD_DIM
MIX_WIDTH = A_WIDTH + B_WIDTH + C_WIDTH
IN_SIZES = (A_HEADS * HEAD_DIM, A_KV * HEAD_DIM, A_KV * HEAD_DIM,
            B_HEADS * HEAD_DIM, B_HEADS * HEAD_DIM, B_HEADS * HEAD_DIM,
            C_HEADS * HEAD_DIM, C_KV * HEAD_DIM, C_KV * HEAD_DIM,
            N_BRANCH * D_MODEL)
IN_COLS = sum(IN_SIZES)

kernel_name = "hybrid_dit_parallel_gated_mixers"


def rmsnorm(x, g):
    x32 = x.astype(jnp.float32)
    y = x32 * lax.rsqrt(jnp.mean(x32 * x32, axis=-1, keepdims=True) + EPS)
    return (y * g.astype(jnp.float32)).astype(x.dtype)


def modulate(xn, shift, scale):
    return xn * (1 + scale) + shift


def axial_angles(n):
    t = jnp.arange(n)
    row = (t // GRID_W).astype(jnp.float32)
    col = (t % GRID_W).astype(jnp.float32)
    quarter = HEAD_DIM // 4
    inv = ROPE_THETA ** (-jnp.arange(quarter, dtype=jnp.float32) / quarter)
    ang_r = row[:, None] * inv[None, :]
    ang_c = col[:, None] * inv[None, :]
    return (jnp.cos(ang_r), jnp.sin(ang_r), jnp.cos(ang_c), jnp.sin(ang_c))


def rotate_half(x, cos, sin):
    m = x.shape[-1] // 2
    x1, x2 = x[..., :m], x[..., m:]
    cos, sin = cos[:, None, :], sin[:, None, :]
    return jnp.concatenate([x1 * cos - x2 * sin, x2 * cos + x1 * sin], axis=-1)


def axial_rope(x, rope):
    cr, sr, cc, sc = rope
    half = HEAD_DIM // 2
    out = jnp.concatenate([rotate_half(x[..., :half], cr, sr),
                           rotate_half(x[..., half:], cc, sc)], axis=-1)
    return out.astype(x.dtype)


def softmax_with_sink(s, sink):
    sink = jnp.broadcast_to(sink.astype(jnp.float32), s.shape[:-1] + (1,))
    m = jnp.maximum(jnp.max(s, axis=-1, keepdims=True), sink)
    e = jnp.exp(s - m)
    return e / (jnp.sum(e, axis=-1, keepdims=True) + jnp.exp(sink - m))


def project(h, w_in):
    parts = jnp.split(h @ w_in, np.cumsum(IN_SIZES)[:-1].tolist(), axis=-1)
    counts = (A_HEADS, A_KV, A_KV, B_HEADS, B_HEADS, B_HEADS, C_HEADS, C_KV, C_KV)
    qkv = [t.reshape(t.shape[:-1] + (n, HEAD_DIM)) for t, n in zip(parts[:-1], counts)]
    return (*qkv, parts[-1])


def dense_ctx_attention(q, k, v, sink=None):
    b, n, h, d = q.shape
    kv = k.shape[2]
    qg = q.reshape(b, n, kv, h // kv, d)
    s = jnp.einsum('bqkgd,bskd->bkgqs', qg, k).astype(jnp.float32) * d ** -0.5
    if sink is None:
        p = jax.nn.softmax(s, axis=-1)
    else:
        p = softmax_with_sink(s, sink.reshape(1, kv, h // kv, 1, 1))
    o = jnp.einsum('bkgqs,bskd->bqkgd', p.astype(v.dtype), v)
    return o.reshape(b, n, h * d)


def windowed_attention(q, k, v, k_ctx, v_ctx, sink):
    b, s, h, d = q.shape
    kv = k.shape[2]
    g = h // kv
    nb = s // BLOCK
    qb = q.reshape(b, nb, BLOCK, kv, g, d)

    def band(t):
        tp = jnp.pad(t, ((0, 0), (BLOCK, BLOCK), (0, 0), (0, 0))).reshape(b, nb + 2, BLOCK, kv, d)
        return jnp.concatenate([tp[:, :-2], tp[:, 1:-1], tp[:, 2:]], axis=2)

    kb, vb = band(k), band(v)
    scale = d ** -0.5
    s_lat = jnp.einsum('bnqkgd,bnskd->bnkgqs', qb, kb).astype(jnp.float32) * scale
    q_pos = jnp.arange(nb)[:, None] * BLOCK + jnp.arange(BLOCK)[None, :]
    k_pos = (jnp.arange(nb)[:, None] - 1) * BLOCK + jnp.arange(3 * BLOCK)[None, :]
    kp = k_pos[:, None, :]
    valid = (kp >= 0) & (kp < s) & (jnp.abs(kp - q_pos[:, :, None]) <= WINDOW)
    s_lat = jnp.where(valid[None, :, None, None], s_lat, NEG)
    s_ctx = jnp.einsum('bnqkgd,bckd->bnkgqc', qb, k_ctx).astype(jnp.float32) * scale
    p = softmax_with_sink(jnp.concatenate([s_lat, s_ctx], axis=-1),
                          sink.reshape(1, 1, kv, g, 1, 1)).astype(v.dtype)
    o = (jnp.einsum('bnkgqs,bnskd->bnqkgd', p[..., :3 * BLOCK], vb)
         + jnp.einsum('bnkgqc,bckd->bnqkgd', p[..., 3 * BLOCK:], v_ctx))
    return o.reshape(b, s, h * d)


def neighbourhood_attention(q, k, v, k_ctx, v_ctx, rpb):
    b, s, h, d = q.shape
    rows = s // GRID_W
    kh = min(NA_H, rows)
    kw = NA_W
    scale = d ** -0.5
    qg = jnp.moveaxis(q.reshape(b, rows, GRID_W, h, d), 1, 0)
    kg = k.reshape(b, rows, GRID_W, h, d)
    vg = v.reshape(b, rows, GRID_W, h, d)
    cols = jnp.arange(GRID_W)
    cs = jnp.clip(cols - kw // 2, 0, GRID_W - kw)
    col_valid = (cols[None, :] >= cs[:, None]) & (cols[None, :] < cs[:, None] + kw)
    mask = jnp.tile(col_valid, (1, kh))
    dc = jnp.clip(cols[None, :] - cols[:, None] + NA_W - 1, 0, 2 * NA_W - 2)

    def row_block(args):
        q_row, r = args
        rs = jnp.clip(r - kh // 2, 0, rows - kh)
        k_strip = lax.dynamic_slice_in_dim(kg, rs, kh, axis=1).reshape(b, kh * GRID_W, h, d)
        v_strip = lax.dynamic_slice_in_dim(vg, rs, kh, axis=1).reshape(b, kh * GRID_W, h, d)
        dr = rs + jnp.arange(kh) - r + NA_H - 1
        bias = rpb[:, dr][:, :, dc]
        bias = jnp.transpose(bias, (0, 2, 1, 3)).reshape(h, GRID_W, kh * GRID_W)
        s_lat = jnp.einsum('bqhd,bshd->bhqs', q_row, k_strip).astype(jnp.float32) * scale
        s_lat = jnp.where(mask, s_lat + bias.astype(jnp.float32), NEG)
        s_ctx = jnp.einsum('bqhd,bchd->bhqc', q_row, k_ctx).astype(jnp.float32) * scale
        p = jax.nn.softmax(jnp.concatenate([s_lat, s_ctx], axis=-1), axis=-1).astype(v.dtype)
        return (jnp.einsum('bhqs,bshd->bqhd', p[..., :kh * GRID_W], v_strip)
                + jnp.einsum('bhqc,bchd->bqhd', p[..., kh * GRID_W:], v_ctx))

    o = lax.map(row_block, (qg, jnp.arange(rows)))
    return jnp.moveaxis(o, 0, 1).reshape(b, s, h * d)


def global_attention(q, k, v, k_ctx, v_ctx):
    b, s, h, d = q.shape
    kv = k.shape[2]
    g = h // kv
    nb = s // BLOCK
    k_all = jnp.concatenate([k_ctx, k], axis=1)
    v_all = jnp.concatenate([v_ctx, v], axis=1)
    qb = jnp.moveaxis(q.reshape(b, nb, BLOCK, kv, g, d), 1, 0)
    scale = d ** -0.5

    def block(qi):
        sc = jnp.einsum('bqkgd,bskd->bkgqs', qi, k_all).astype(jnp.float32) * scale
        p = jax.nn.softmax(sc, axis=-1).astype(v.dtype)
        return jnp.einsum('bkgqs,bskd->bqkgd', p, v_all)

    o = lax.map(block, qb)
    return jnp.moveaxis(o, 0, 1).reshape(b, s, h * d)


def merge_branches(o_a, o_b, o_c, gates, w_branch, w_out):
    ga, gb, gc = jnp.split(jax.nn.sigmoid(gates), N_BRANCH, axis=-1)
    br_a = o_a @ w_branch[:A_WIDTH]
    br_b = o_b @ w_branch[A_WIDTH:A_WIDTH + B_WIDTH]
    br_c = o_c @ w_branch[A_WIDTH + B_WIDTH:]
    return (ga * br_a + gb * br_b + gc * br_c) @ w_out


def conv_ffn(h, w_up, conv_w, conv_b, w_down):
    a, u = jnp.split(h @ w_up, 2, axis=-1)
    n = a.shape[1]
    pad = CONV_W // 2
    ap = jnp.pad(a, ((0, 0), (pad, pad), (0, 0)))
    a = sum(ap[:, i:i + n] * conv_w[i] for i in range(CONV_W)) + conv_b
    return (jax.nn.silu(a) * u) @ w_down


def setup_inputs(seed: int = 0) -> dict:
    key = jax.random.key(seed)
    ks = jax.random.split(key, 20)
    D = D_MODEL

    def nrm(k, shape, scale):
        return jax.random.normal(k, shape, jnp.float32) * scale

    return {
        "x": nrm(ks[0], (BATCH, SEQ, D), 1.0),
        "c": nrm(ks[1], (BATCH, D), 1.0),
        "ctx": nrm(ks[2], (BATCH, CTX_LEN, D), 1.0),
        "c_ctx": nrm(ks[3], (D,), 1.0),
        "w_ada": nrm(ks[4], (DEPTH, D, 6 * D), 0.5 * D ** -0.5),
        "b_ada": nrm(ks[5], (DEPTH, 6 * D), 0.01),
        "norm1": 1.0 + nrm(ks[6], (DEPTH, D), 0.05),
        "w_in": nrm(ks[7], (DEPTH, D, IN_COLS), D ** -0.5),
        "sink_a": nrm(ks[8], (DEPTH, A_HEADS), 0.5),
        "rpb_b": nrm(ks[9], (DEPTH, B_HEADS, 2 * NA_H - 1, 2 * NA_W - 1), 0.5),
        "qnorm_c": 1.0 + nrm(ks[10], (DEPTH, HEAD_DIM), 0.05),
        "knorm_c": 1.0 + nrm(ks[11], (DEPTH, HEAD_DIM), 0.05),
        "w_branch": nrm(ks[12], (DEPTH, MIX_WIDTH, D), B_WIDTH ** -0.5),
        "w_out": nrm(ks[13], (DEPTH, D, D), D ** -0.5),
        "norm2": 1.0 + nrm(ks[14], (DEPTH, D), 0.05),
        "w_up": nrm(ks[15], (DEPTH, D, 2 * D_FF), D ** -0.5),
        "conv_w": nrm(ks[16], (DEPTH, CONV_W, D_FF), CONV_W ** -0.5),
        "conv_b": nrm(ks[17], (DEPTH, D_FF), 0.01),
        "w_down": nrm(ks[18], (DEPTH, D_FF, D), D_FF ** -0.5),
        "final_norm": 1.0 + nrm(ks[19], (D,), 0.05),
    }


def reference(x, c, ctx, c_ctx, w_ada, b_ada, norm1, w_in, sink_a, rpb_b, qnorm_c, knorm_c,
              w_branch, w_out, norm2, w_up, conv_w, conv_b, w_down, final_norm):
    rope = axial_angles(x.shape[1])
    xc = ctx
    silu_c = jax.nn.silu(c)
    silu_cc = jax.nn.silu(c_ctx)
    for l in range(DEPTH):
        last = l == DEPTH - 1
        mod = jnp.split((silu_c @ w_ada[l] + b_ada[l])[:, None, :], 6, axis=-1)
        mod_c = jnp.split(silu_cc @ w_ada[l] + b_ada[l], 6, axis=-1)

        h = modulate(rmsnorm(x, norm1[l]), mod[0], mod[1])
        hc = modulate(rmsnorm(xc, norm1[l]), mod_c[0], mod_c[1])
        qa, ka, va, qb, kb, vb, qc, kc, vc, gates = project(h, w_in[l])
        qa_x, ka_x, va_x, qb_x, kb_x, vb_x, qc_x, kc_x, vc_x, gates_x = project(hc, w_in[l])
        qc = rmsnorm(qc, qnorm_c[l])
        kc = rmsnorm(kc, knorm_c[l])
        kc_x = rmsnorm(kc_x, knorm_c[l])

        o_a = windowed_attention(axial_rope(qa, rope), axial_rope(ka, rope), va, ka_x, va_x, sink_a[l])
        o_b = neighbourhood_attention(qb, kb, vb, kb_x, vb_x, rpb_b[l])
        o_c = global_attention(axial_rope(qc, rope), axial_rope(kc, rope), vc, kc_x, vc_x)
        x = x + mod[2] * merge_branches(o_a, o_b, o_c, gates, w_branch[l], w_out[l])

        h2 = modulate(rmsnorm(x, norm2[l]), mod[3], mod[4])
        x = x + mod[5] * conv_ffn(h2, w_up[l], conv_w[l], conv_b[l], w_down[l])

        if not last:
            oa_x = dense_ctx_attention(qa_x, ka_x, va_x, sink_a[l])
            ob_x = dense_ctx_attention(qb_x, kb_x, vb_x)
            oc_x = dense_ctx_attention(rmsnorm(qc_x, qnorm_c[l]), kc_x, vc_x)
            xc = xc + mod_c[2] * merge_branches(oa_x, ob_x, oc_x, gates_x, w_branch[l], w_out[l])
            hc2 = modulate(rmsnorm(xc, norm2[l]), mod_c[3], mod_c[4])
            xc = xc + mod_c[5] * conv_ffn(hc2, w_up[l], conv_w[l], conv_b[l], w_down[l])
    return rmsnorm(x, final_norm)
```

```python
import functools

import jax
import jax.numpy as jnp
from jax import lax
from jax.experimental import pallas as pl
from jax.experimental.pallas import tpu as pltpu

F32 = jnp.float32
BF16 = jnp.bfloat16

GRID_W = 64
HEAD_DIM = 128
A_HEADS, A_KV = 6, 2
B_HEADS = 4
C_HEADS, C_KV = 6, 2
WINDOW = 128
NA_H, NA_W = 8, 16
ROPE_THETA = 10000.0
CONV_W = 3
N_BRANCH = 3
EPS = 1e-6
NEG = -1e30
SCALE = HEAD_DIM ** -0.5
A_WIDTH = A_HEADS * HEAD_DIM
B_WIDTH = B_HEADS * HEAD_DIM
C_WIDTH = C_HEADS * HEAD_DIM
QKV_SIZES = (A_HEADS * HEAD_DIM, A_KV * HEAD_DIM, A_KV * HEAD_DIM,
             B_HEADS * HEAD_DIM, B_HEADS * HEAD_DIM, B_HEADS * HEAD_DIM,
             C_HEADS * HEAD_DIM, C_KV * HEAD_DIM, C_KV * HEAD_DIM)
QKV_COLS = sum(QKV_SIZES)
QKV_KIND = ((True, False, True), (False, False, True), (False, False, False),
            (True, False, False), (False, False, False), (False, False, False),
            (True, True, True), (False, True, True), (False, False, False))
MOD_ROWS = 8
NB_ROWS = 8
NB_KROWS = 16
V7X_VMEM_BYTES = 64 * 1024 * 1024
VMEM_LIMIT = V7X_VMEM_BYTES - 8 * 1024 * 1024


def _cparams(*sem):
    return pltpu.CompilerParams(dimension_semantics=sem, vmem_limit_bytes=VMEM_LIMIT)


def _dot(a, b):
    return jnp.dot(a, b, preferred_element_type=F32)


def _dot_t(a, b):
    return lax.dot_general(a, b, (((1,), (1,)), ((), ())), preferred_element_type=F32)


def _rms(x, g):
    return x * lax.rsqrt(jnp.mean(x * x, axis=-1, keepdims=True) + EPS) * g


def _ada_kernel(s_ref, w_ref, b_ref, o_ref):
    s = s_ref[...]
    s = s * jax.nn.sigmoid(s)
    o_ref[0] = _dot(s.astype(BF16), w_ref[0].astype(BF16)) + b_ref[0]


def _ada(s, w_ada, b_ada):
    depth, d, n = w_ada.shape
    bn = 1024
    return pl.pallas_call(
        _ada_kernel,
        grid=(depth, n // bn),
        in_specs=[pl.BlockSpec((MOD_ROWS, d), lambda l, j: (0, 0)),
                  pl.BlockSpec((1, d, bn), lambda l, j: (l, 0, j)),
                  pl.BlockSpec((1, 1, bn), lambda l, j: (l, 0, j))],
        out_specs=pl.BlockSpec((1, MOD_ROWS, bn), lambda l, j: (l, 0, j)),
        out_shape=jax.ShapeDtypeStruct((depth, MOD_ROWS, n), F32),
        compiler_params=_cparams("parallel", "parallel"),
        name="ada",
    )(s, w_ada, b_ada.reshape(depth, 1, n))


def _bias_table_kernel(rpb_ref, o_ref):
    l, h = pl.program_id(0), pl.program_id(1)
    n_dr, n_dc = 2 * NA_H - 1, 2 * NA_W - 1
    base = (l * B_HEADS + h) * n_dr * n_dc
    qc = lax.broadcasted_iota(jnp.int32, (GRID_W, GRID_W), 0)
    kc = lax.broadcasted_iota(jnp.int32, (GRID_W, GRID_W), 1)
    cs = jnp.clip(qc - NA_W // 2, 0, GRID_W - NA_W)
    col_valid = (kc >= cs) & (kc < cs + NA_W)
    dc = jnp.clip(kc - qc + NA_W - 1, 0, 2 * NA_W - 2)
    neg = jnp.full((GRID_W, GRID_W), NEG, F32)
    toe = []
    for d in range(n_dr):
        t = jnp.zeros((GRID_W, GRID_W), F32)
        for e in range(n_dc):
            t = jnp.where(dc == e, rpb_ref[base + d * n_dc + e], t)
        toe.append(jnp.where(col_valid, t, neg))
    half = NA_H // 2
    for typ in range(3):
        for a in range(NB_ROWS):
            if typ == 0:
                rs_rel, dr0 = max(a - half, 0), NA_H - 1 - a
            elif typ == 1:
                rs_rel, dr0 = a, NA_H - 1 - half - a
            else:
                rs_rel, dr0 = half + min(a, half), NA_H - 1 - NB_ROWS - a
            for jj in range(NB_KROWS):
                valid = rs_rel <= jj < rs_rel + NA_H
                o_ref[0, 0, typ, a * GRID_W:(a + 1) * GRID_W, jj * GRID_W:(jj + 1) * GRID_W] = (
                    toe[jj + dr0] if valid else neg)


def _bias_table(rpb):
    depth = rpb.shape[0]
    shape = (depth, B_HEADS, 3, NB_ROWS * GRID_W, NB_KROWS * GRID_W)
    return pl.pallas_call(
        _bias_table_kernel,
        grid=(depth, B_HEADS),
        in_specs=[pl.BlockSpec(memory_space=pltpu.SMEM)],
        out_specs=pl.BlockSpec((1, 1) + shape[2:], lambda l, h: (l, h, 0, 0, 0)),
        out_shape=jax.ShapeDtypeStruct(shape, F32),
        compiler_params=_cparams("parallel", "parallel"),
        name="bias_table",
    )(rpb.reshape(-1))


class _Stream:
    def __init__(self, n_rows, seq, bm, mod_row):
        assert seq % bm == 0 and n_rows % seq == 0
        self.n_rows, self.seq, self.bm, self.mod_row = n_rows, seq, bm, mod_row
        self.n_blocks = n_rows // bm

    def mod_spec(self, d, chunk):
        return pl.BlockSpec((1, 1, d), lambda i, *_: (self.mod_row(i * self.bm), 0, chunk))


def _modulated_norm(x, g_ref, sh_ref, sc_ref):
    return (_rms(x, g_ref[...]) * (1 + sc_ref[0]) + sh_ref[0]).astype(BF16)


def _prenorm_kernel(x_ref, g_ref, sh_ref, sc_ref, o_ref):
    o_ref[...] = _modulated_norm(x_ref[...], g_ref, sh_ref, sc_ref)


def _prenorm(st, x, g, mod):
    d = x.shape[1]
    row = pl.BlockSpec((st.bm, d), lambda i: (i, 0))
    return pl.pallas_call(
        _prenorm_kernel,
        grid=(st.n_blocks,),
        in_specs=[row, pl.BlockSpec((1, d), lambda i: (0, 0)), st.mod_spec(d, 0), st.mod_spec(d, 1)],
        out_specs=row,
        out_shape=jax.ShapeDtypeStruct(x.shape, BF16),
        compiler_params=_cparams("parallel"),
        name="prenorm",
    )(x, g.reshape(1, d), mod, mod)


def _swap_half_pairs(x):
    lane = lax.broadcasted_iota(jnp.int32, (1, HEAD_DIM), 1)
    first = (lane % (HEAD_DIM // 2)) < HEAD_DIM // 4
    return jnp.where(first, pltpu.roll(x, HEAD_DIM - HEAD_DIM // 4, 1), pltpu.roll(x, HEAD_DIM // 4, 1))


def _inproj_kernel(*refs, rope):
    if rope:
        h_ref, w_ref, qn_ref, kn_ref, cos_ref, sin_ref = refs[:6]
        out_refs = refs[6:]
    else:
        h_ref, w_ref, qn_ref, kn_ref = refs[:4]
        out_refs = refs[4:]
    h = h_ref[...]
    col = 0
    for (is_q, has_norm, has_rope), width, o_ref in zip(QKV_KIND, QKV_SIZES, out_refs):
        for c in range(0, width, 2 * HEAD_DIM):
            y = _dot(h, w_ref[:, col + c:col + c + 2 * HEAD_DIM])
            for s in range(2):
                yh = y[:, s * HEAD_DIM:(s + 1) * HEAD_DIM]
                if has_norm:
                    yh = _rms(yh, (qn_ref if is_q else kn_ref)[...])
                if has_rope and rope:
                    yh = yh * cos_ref[...] + _swap_half_pairs(yh) * sin_ref[...]
                if is_q:
                    yh = yh * SCALE
                o_ref[:, c + s * HEAD_DIM:c + (s + 1) * HEAD_DIM] = yh.astype(BF16)
        col += width


def _inproj(st, h, w, qn, kn, rope_tabs):
    d = h.shape[1]
    rope = rope_tabs is not None
    row = lambda width: pl.BlockSpec((st.bm, width), lambda i: (i, 0))
    vec = pl.BlockSpec((1, HEAD_DIM), lambda i: (0, 0))
    in_specs = [row(d), pl.BlockSpec((d, QKV_COLS), lambda i: (0, 0), pipeline_mode=pl.Buffered(1)), vec, vec]
    args = [h, w, qn.reshape(1, HEAD_DIM), kn.reshape(1, HEAD_DIM)]
    if rope:
        per_seq = st.seq // st.bm
        tab = pl.BlockSpec((st.bm, HEAD_DIM), lambda i: (i % per_seq, 0))
        in_specs += [tab, tab]
        args += list(rope_tabs)
    return pl.pallas_call(
        functools.partial(_inproj_kernel, rope=rope),
        grid=(st.n_blocks,),
        in_specs=in_specs,
        out_specs=[row(width) for width in QKV_SIZES],
        out_shape=[jax.ShapeDtypeStruct((st.n_rows, width), BF16) for width in QKV_SIZES],
        compiler_params=_cparams("parallel"),
        name="inproj",
    )(*args)


def _softmax_pv(s, sx, v, vx, sink=None):
    m = jnp.maximum(jnp.max(s, axis=-1, keepdims=True), jnp.max(sx, axis=-1, keepdims=True))
    if sink is not None:
        m = jnp.maximum(m, sink)
    e = jnp.exp(s - m)
    ex = jnp.exp(sx - m)
    den = jnp.sum(e, axis=-1, keepdims=True) + jnp.sum(ex, axis=-1, keepdims=True)
    if sink is not None:
        den = den + jnp.exp(sink - m)
    o = _dot(e.astype(BF16), v) + _dot(ex.astype(BF16), vx)
    return o / den


def _attn_a_kernel(sink_ref, q_ref, k_ref, v_ref, kx_ref, vx_ref, o_ref, *, bq, seq):
    kv, i = pl.program_id(1), pl.program_id(2)
    wlen = bq + 2 * WINDOW
    q0 = i * bq
    start = pl.multiple_of(jnp.clip(q0 - WINDOW, 0, seq - wlen), WINDOW)
    kw = k_ref[pl.ds(start, wlen), :]
    vw = v_ref[pl.ds(start, wlen), :]
    qpos = q0 + lax.broadcasted_iota(jnp.int32, (bq, 1), 0)
    kpos = start + lax.broadcasted_iota(jnp.int32, (1, wlen), 1)
    valid = jnp.abs(kpos - qpos) <= WINDOW
    kx, vx = kx_ref[...], vx_ref[...]
    group = A_HEADS // A_KV
    for g in range(group):
        q = q_ref[:, g * HEAD_DIM:(g + 1) * HEAD_DIM]
        s = jnp.where(valid, _dot_t(q, kw), NEG)
        o = _softmax_pv(s, _dot_t(q, kx), vw, vx, sink_ref[kv * group + g])
        o_ref[:, g * HEAD_DIM:(g + 1) * HEAD_DIM] = o.astype(BF16)


def _attn_a(q, k, v, kx, vx, sink, batch, seq, ctx_len):
    bq = 256
    assert seq % bq == 0 and seq >= bq + 2 * WINDOW
    nq = seq // bq
    gw = (A_HEADS // A_KV) * HEAD_DIM
    qspec = pl.BlockSpec((bq, gw), lambda b, kv, i: (b * nq + i, kv))
    kvspec = pl.BlockSpec((seq, HEAD_DIM), lambda b, kv, i: (b, kv))
    xspec = pl.BlockSpec((ctx_len, HEAD_DIM), lambda b, kv, i: (b, kv))
    return pl.pallas_call(
        functools.partial(_attn_a_kernel, bq=bq, seq=seq),
        grid=(batch, A_KV, nq),
        in_specs=[pl.BlockSpec(memory_space=pltpu.SMEM), qspec, kvspec, kvspec, xspec, xspec],
        out_specs=qspec,
        out_shape=jax.ShapeDtypeStruct(q.shape, BF16),
        compiler_params=_cparams("parallel", "parallel", "parallel"),
        name="attn_window",
    )(sink, q, k, v, kx, vx)


def _attn_b_kernel(q_ref, k_ref, v_ref, kx_ref, vx_ref, tab_ref, o_ref, *, rows):
    rb = pl.program_id(2)
    ks = jnp.clip(rb * NB_ROWS - NA_H // 2, 0, rows - NB_KROWS)
    start = pl.multiple_of(ks * GRID_W, GRID_W)
    n_keys = NB_KROWS * GRID_W
    q = q_ref[...]
    s = _dot_t(q, k_ref[pl.ds(start, n_keys), :]) + tab_ref[0, 0]
    o = _softmax_pv(s, _dot_t(q, kx_ref[...]), v_ref[pl.ds(start, n_keys), :], vx_ref[...])
    o_ref[...] = o.astype(BF16)


def _attn_b(q, k, v, kx, vx, table, batch, seq, ctx_len):
    rows = seq // GRID_W
    assert seq % GRID_W == 0 and rows % NB_ROWS == 0 and rows >= NB_KROWS
    nrb = rows // NB_ROWS
    bq = NB_ROWS * GRID_W
    qspec = pl.BlockSpec((bq, HEAD_DIM), lambda h, b, rb: (b * nrb + rb, h))
    kvspec = pl.BlockSpec((seq, HEAD_DIM), lambda h, b, rb: (b, h))
    xspec = pl.BlockSpec((ctx_len, HEAD_DIM), lambda h, b, rb: (b, h))
    tspec = pl.BlockSpec(
        (1, 1, bq, NB_KROWS * GRID_W),
        lambda h, b, rb: (h, jnp.where(rb == 0, 0, jnp.where(rb == nrb - 1, 2, 1)), 0, 0))
    return pl.pallas_call(
        functools.partial(_attn_b_kernel, rows=rows),
        grid=(B_HEADS, batch, nrb),
        in_specs=[qspec, kvspec, kvspec, xspec, xspec, tspec],
        out_specs=qspec,
        out_shape=jax.ShapeDtypeStruct(q.shape, BF16),
        compiler_params=_cparams("parallel", "parallel", "parallel"),
        name="attn_neighbourhood",
    )(q, k, v, kx, vx, table)


def _attn_c_kernel(q_ref, k_ref, v_ref, kx_ref, vx_ref, o_ref):
    k, v, kx, vx = k_ref[...], v_ref[...], kx_ref[...], vx_ref[...]
    for g in range(C_HEADS // C_KV):
        q = q_ref[:, g * HEAD_DIM:(g + 1) * HEAD_DIM]
        o = _softmax_pv(_dot_t(q, k), _dot_t(q, kx), v, vx)
        o_ref[:, g * HEAD_DIM:(g + 1) * HEAD_DIM] = o.astype(BF16)


def _attn_c(q, k, v, kx, vx, batch, seq, ctx_len):
    bq = 256
    assert seq % bq == 0
    nq = seq // bq
    gw = (C_HEADS // C_KV) * HEAD_DIM
    qspec = pl.BlockSpec((bq, gw), lambda b, kv, i: (b * nq + i, kv))
    kvspec = pl.BlockSpec((seq, HEAD_DIM), lambda b, kv, i: (b, kv))
    xspec = pl.BlockSpec((ctx_len, HEAD_DIM), lambda b, kv, i: (b, kv))
    return pl.pallas_call(
        _attn_c_kernel,
        grid=(batch, C_KV, nq),
        in_specs=[qspec, kvspec, kvspec, xspec, xspec],
        out_specs=qspec,
        out_shape=jax.ShapeDtypeStruct(q.shape, BF16),
        compiler_params=_cparams("parallel", "parallel", "parallel"),
        name="attn_global",
    )(q, k, v, kx, vx)


def _softmax_pv_dense(s, v, sink=None):
    m = jnp.max(s, axis=-1, keepdims=True)
    if sink is not None:
        m = jnp.maximum(m, sink)
    e = jnp.exp(s - m)
    den = jnp.sum(e, axis=-1, keepdims=True)
    if sink is not None:
        den = den + jnp.exp(sink - m)
    return _dot(e.astype(BF16), v) / den


def _attn_ctx_kernel(sink_ref, qa_ref, ka_ref, va_ref, qb_ref, kb_ref, vb_ref, qc_ref, kc_ref, vc_ref,
                     oa_ref, ob_ref, oc_ref):
    def head(ref, h):
        return ref[:, h * HEAD_DIM:(h + 1) * HEAD_DIM]

    def put(ref, h, o):
        ref[:, h * HEAD_DIM:(h + 1) * HEAD_DIM] = o.astype(BF16)

    ga, gc = A_HEADS // A_KV, C_HEADS // C_KV
    for h in range(A_HEADS):
        s = _dot_t(head(qa_ref, h), head(ka_ref, h // ga))
        put(oa_ref, h, _softmax_pv_dense(s, head(va_ref, h // ga), sink_ref[h]))
    for h in range(B_HEADS):
        put(ob_ref, h, _softmax_pv_dense(_dot_t(head(qb_ref, h), head(kb_ref, h)), head(vb_ref, h)))
    for h in range(C_HEADS):
        s = _dot_t(head(qc_ref, h), head(kc_ref, h // gc))
        put(oc_ref, h, _softmax_pv_dense(s, head(vc_ref, h // gc)))


def _attn_ctx(qkv, sink, batch, ctx_len):
    spec = lambda width: pl.BlockSpec((ctx_len, width), lambda b: (b, 0))
    out_w = (A_WIDTH, B_WIDTH, C_WIDTH)
    return pl.pallas_call(
        _attn_ctx_kernel,
        grid=(batch,),
        in_specs=[pl.BlockSpec(memory_space=pltpu.SMEM)] + [spec(w) for w in QKV_SIZES],
        out_specs=[spec(w) for w in out_w],
        out_shape=[jax.ShapeDtypeStruct((batch * ctx_len, w), BF16) for w in out_w],
        compiler_params=_cparams("parallel"),
        name="attn_ctx",
    )(sink, *qkv)


def _merge_kernel(x_ref, h_ref, oa_ref, ob_ref, oc_ref, wga_ref, wgb_ref, wgc_ref, wb_ref, wo_ref,
                  gate_ref, g2_ref, sh2_ref, sc2_ref, xo_ref, h2_ref, acc_ref):
    j = pl.program_id(1)

    @pl.when(j == 0)
    def _():
        acc_ref[...] = jnp.zeros_like(acc_ref)

    h = h_ref[...]
    m = (jax.nn.sigmoid(_dot(h, wga_ref[...])) * _dot(oa_ref[...], wb_ref[:A_WIDTH, :])
         + jax.nn.sigmoid(_dot(h, wgb_ref[...])) * _dot(ob_ref[...], wb_ref[A_WIDTH:A_WIDTH + B_WIDTH, :])
         + jax.nn.sigmoid(_dot(h, wgc_ref[...])) * _dot(oc_ref[...], wb_ref[A_WIDTH + B_WIDTH:, :]))
    acc_ref[...] += _dot(m.astype(BF16), wo_ref[...])

    @pl.when(j == pl.num_programs(1) - 1)
    def _():
        xn = x_ref[...] + gate_ref[0] * acc_ref[...]
        xo_ref[...] = xn
        h2_ref[...] = _modulated_norm(xn, g2_ref, sh2_ref, sc2_ref)


def _merge(st, x, h, o_a, o_b, o_c, wg, wb, wo, mod, g2):
    d = x.shape[1]
    bj = 256
    nj = d // bj
    row = lambda width: pl.BlockSpec((st.bm, width), lambda i, j: (i, 0))
    gspec = lambda k: pl.BlockSpec((d, bj), lambda i, j: (0, k * nj + j))
    return pl.pallas_call(
        _merge_kernel,
        grid=(st.n_blocks, nj),
        in_specs=[row(d), row(d), row(A_WIDTH), row(B_WIDTH), row(C_WIDTH),
                  gspec(0), gspec(1), gspec(2),
                  pl.BlockSpec((wb.shape[0], bj), lambda i, j: (0, j)),
                  pl.BlockSpec((bj, d), lambda i, j: (j, 0)),
                  st.mod_spec(d, 2), pl.BlockSpec((1, d), lambda i, j: (0, 0)),
                  st.mod_spec(d, 3), st.mod_spec(d, 4)],
        out_specs=[row(d), row(d)],
        out_shape=[jax.ShapeDtypeStruct(x.shape, F32), jax.ShapeDtypeStruct(x.shape, BF16)],
        scratch_shapes=[pltpu.VMEM((st.bm, d), F32)],
        compiler_params=_cparams("parallel", "arbitrary"),
        name="merge",
    )(x, h, o_a, o_b, o_c, wg, wg, wg, wb, wo, mod, g2.reshape(1, d), mod, mod)


HALO = 16


def _ffn_kernel(*refs, bm, seq, last):
    (x_ref, h_ref, hp_ref, hn_ref, wa_ref, wu_ref, cw_ref, cb_ref, wd_ref, gate_ref, gn_ref) = refs[:11]
    if last:
        out_ref, acc_ref, hext_ref = refs[11:]
    else:
        shn_ref, scn_ref, xo_ref, hn_out_ref, acc_ref, hext_ref = refs[11:]
    i, j = pl.program_id(0), pl.program_id(1)

    @pl.when(j == 0)
    def _():
        acc_ref[...] = jnp.zeros_like(acc_ref)
        pos = (i * bm) % seq
        zeros = jnp.zeros_like(hp_ref)
        hext_ref[:HALO, :] = jnp.where(pos == 0, zeros, hp_ref[...])
        hext_ref[HALO:HALO + bm, :] = h_ref[...]
        hext_ref[HALO + bm:, :] = jnp.where(pos + bm == seq, zeros, hn_ref[...])

    a_ext = _dot(hext_ref[...], wa_ref[...])
    u = _dot(h_ref[...], wu_ref[...])
    a = (a_ext[HALO - 1:HALO - 1 + bm] * cw_ref[0:1, :] + a_ext[HALO:HALO + bm] * cw_ref[1:2, :]
         + a_ext[HALO + 1:HALO + 1 + bm] * cw_ref[2:3, :] + cb_ref[...])
    g = a * jax.nn.sigmoid(a) * u
    acc_ref[...] += _dot(g.astype(BF16), wd_ref[...])

    @pl.when(j == pl.num_programs(1) - 1)
    def _():
        xn = x_ref[...] + gate_ref[0] * acc_ref[...]
        if last:
            out_ref[...] = _rms(xn, gn_ref[...])
        else:
            xo_ref[...] = xn
            hn_out_ref[...] = _modulated_norm(xn, gn_ref, shn_ref, scn_ref)


def _ffn(st, x, h2, wup, cw, cb, wd, mod, g_next, mod_next):
    d = x.shape[1]
    ff = wd.shape[0]
    bn = 512
    assert ff % bn == 0 and st.bm % HALO == 0
    nj = ff // bn
    last = mod_next is None
    per_blk = st.bm // HALO
    n_halo = st.n_rows // HALO
    row = pl.BlockSpec((st.bm, d), lambda i, j: (i, 0))
    in_specs = [row, row,
                pl.BlockSpec((HALO, d), lambda i, j: (jnp.maximum(i * per_blk - 1, 0), 0)),
                pl.BlockSpec((HALO, d), lambda i, j: (jnp.minimum((i + 1) * per_blk, n_halo - 1), 0)),
                pl.BlockSpec((d, bn), lambda i, j: (0, j)),
                pl.BlockSpec((d, bn), lambda i, j: (0, nj + j)),
                pl.BlockSpec((CONV_W, bn), lambda i, j: (0, j)),
                pl.BlockSpec((1, bn), lambda i, j: (0, j)),
                pl.BlockSpec((bn, d), lambda i, j: (j, 0)),
                st.mod_spec(d, 5),
                pl.BlockSpec((1, d), lambda i, j: (0, 0))]
    args = [x, h2, h2, h2, wup, wup, cw, cb.reshape(1, ff), wd, mod, g_next.reshape(1, d)]
    if last:
        out_specs = row
        out_shape = jax.ShapeDtypeStruct(x.shape, F32)
    else:
        nxt = _Stream(st.n_rows, st.seq, st.bm, st.mod_row)
        in_specs += [nxt.mod_spec(d, 0), nxt.mod_spec(d, 1)]
        args += [mod_next, mod_next]
        out_specs = [row, row]
        out_shape = [jax.ShapeDtypeStruct(x.shape, F32), jax.ShapeDtypeStruct(x.shape, BF16)]
    return pl.pallas_call(
        functools.partial(_ffn_kernel, bm=st.bm, seq=st.seq, last=last),
        grid=(st.n_blocks, nj),
        in_specs=in_specs,
        out_specs=out_specs,
        out_shape=out_shape,
        scratch_shapes=[pltpu.VMEM((st.bm, d), F32), pltpu.VMEM((st.bm + 2 * HALO, d), BF16)],
        compiler_params=_cparams("parallel", "arbitrary"),
        name="convglu",
    )(*args)


def _rope_tables(seq):
    t = jnp.arange(seq)
    row = (t // GRID_W).astype(F32)
    col = (t % GRID_W).astype(F32)
    quarter = HEAD_DIM // 4
    inv = ROPE_THETA ** (-jnp.arange(quarter, dtype=F32) / quarter)
    ang_r = row[:, None] * inv[None, :]
    ang_c = col[:, None] * inv[None, :]
    cr, sr, cc, sc = jnp.cos(ang_r), jnp.sin(ang_r), jnp.cos(ang_c), jnp.sin(ang_c)
    return (jnp.concatenate([cr, cr, cc, cc], axis=-1), jnp.concatenate([-sr, sr, -sc, sc], axis=-1))


def kernel(x, c, ctx, c_ctx, w_ada, b_ada, norm1, w_in, sink_a, rpb_b, qnorm_c, knorm_c, w_branch, w_out,
           norm2, w_up, conv_w, conv_b, w_down, final_norm):
    batch, seq, d = x.shape
    ctx_len = ctx.shape[1]
    depth = w_ada.shape[0]
    assert batch + 1 <= MOD_ROWS

    lat = _Stream(batch * seq, seq, 512, lambda r: r // seq)
    cst = _Stream(batch * ctx_len, ctx_len, ctx_len, lambda r: batch)

    s = jnp.concatenate([c, c_ctx[None, :], jnp.zeros((MOD_ROWS - batch - 1, d), F32)], axis=0)
    mods = _ada(s, w_ada, b_ada).reshape(depth, MOD_ROWS, 1, 6 * d)
    table = _bias_table(rpb_b)
    rope_tabs = _rope_tables(seq)

    xl = x.reshape(batch * seq, d)
    xc = ctx.reshape(batch * ctx_len, d)
    h = _prenorm(lat, xl, norm1[0], mods[0])
    hc = _prenorm(cst, xc, norm1[0], mods[0])
    out = None
    for l in range(depth):
        last = l == depth - 1
        w_qkv = w_in[l, :, :QKV_COLS].astype(BF16)
        w_gate = w_in[l, :, QKV_COLS:].astype(BF16)
        wb, wo = w_branch[l].astype(BF16), w_out[l].astype(BF16)
        wup, wd = w_up[l].astype(BF16), w_down[l].astype(BF16)

        qa, ka, va, qb, kb, vb, qc, kc, vc = _inproj(lat, h, w_qkv, qnorm_c[l], knorm_c[l], rope_tabs)
        qkv_x = _inproj(cst, hc, w_qkv, qnorm_c[l], knorm_c[l], None)
        _, ka_x, va_x, _, kb_x, vb_x, _, kc_x, vc_x = qkv_x
        o_a = _attn_a(qa, ka, va, ka_x, va_x, sink_a[l], batch, seq, ctx_len)
        o_b = _attn_b(qb, kb, vb, kb_x, vb_x, table[l], batch, seq, ctx_len)
        o_c = _attn_c(qc, kc, vc, kc_x, vc_x, batch, seq, ctx_len)
        xl, h2 = _merge(lat, xl, h, o_a, o_b, o_c, w_gate, wb, wo, mods[l], norm2[l])
        if last:
            out = _ffn(lat, xl, h2, wup, conv_w[l], conv_b[l], wd, mods[l], final_norm, None)
        else:
            xl, h = _ffn(lat, xl, h2, wup, conv_w[l], conv_b[l], wd, mods[l], norm1[l + 1], mods[l + 1])
            oa_x, ob_x, oc_x = _attn_ctx(qkv_x, sink_a[l], batch, ctx_len)
            xc, hc2 = _merge(cst, xc, hc, oa_x, ob_x, oc_x, w_gate, wb, wo, mods[l], norm2[l])
            xc, hc = _ffn(cst, xc, hc2, wup, conv_w[l], conv_b[l], wd, mods[l], norm1[l + 1], mods[l + 1])
    return out.reshape(batch, seq, d)
```

```python
import functools

import jax
import jax.numpy as jnp
from jax import lax
from jax.experimental import pallas as pl
from jax.experimental.pallas import tpu as pltpu

F32 = jnp.float32
BF16 = jnp.bfloat16

GRID_W = 64
HEAD_DIM = 128
A_HEADS, A_KV = 6, 2
B_HEADS = 4
C_HEADS, C_KV = 6, 2
WINDOW = 128
NA_H, NA_W = 8, 16
ROPE_THETA = 10000.0
CONV_W = 3
N_BRANCH = 3
EPS = 1e-6
NEG = -1e30
LOG2E = 1.4426950408889634
QSCALE = HEAD_DIM ** -0.5 * LOG2E
A_WIDTH = A_HEADS * HEAD_DIM
B_WIDTH = B_HEADS * HEAD_DIM
C_WIDTH = C_HEADS * HEAD_DIM
QKV_SIZES = (A_HEADS * HEAD_DIM, A_KV * HEAD_DIM, A_KV * HEAD_DIM,
             B_HEADS * HEAD_DIM, B_HEADS * HEAD_DIM, B_HEADS * HEAD_DIM,
             C_HEADS * HEAD_DIM, C_KV * HEAD_DIM, C_KV * HEAD_DIM)
QKV_COLS = sum(QKV_SIZES)
QKV_KIND = ((True, False, True), (False, False, True), (False, False, False),
            (True, False, False), (False, False, False), (False, False, False),
            (True, True, True), (False, True, True), (False, False, False))
MOD_ROWS = 8
NB_ROWS = 8
NB_KROWS = 16
V7X_VMEM_BYTES = 64 * 1024 * 1024
VMEM_LIMIT = V7X_VMEM_BYTES - 8 * 1024 * 1024


def _cparams(*sem):
    return pltpu.CompilerParams(dimension_semantics=sem, vmem_limit_bytes=VMEM_LIMIT)


def _dot(a, b):
    return jnp.dot(a, b, preferred_element_type=F32)


def _dot_t(a, b):
    return lax.dot_general(a, b, (((1,), (1,)), ((), ())), preferred_element_type=F32)


def _rms(x, g):
    return x * lax.rsqrt(jnp.mean(x * x, axis=-1, keepdims=True) + EPS) * g


def _ada_kernel(s_ref, w_ref, b_ref, o_ref):
    s = s_ref[...]
    s = s * jax.nn.sigmoid(s)
    o_ref[0] = _dot(s.astype(BF16), w_ref[0].astype(BF16)) + b_ref[0]


def _ada(s, w_ada, b_ada):
    depth, d, n = w_ada.shape
    bn = 1024
    return pl.pallas_call(
        _ada_kernel,
        grid=(depth, n // bn),
        in_specs=[pl.BlockSpec((MOD_ROWS, d), lambda l, j: (0, 0)),
                  pl.BlockSpec((1, d, bn), lambda l, j: (l, 0, j)),
                  pl.BlockSpec((1, 1, bn), lambda l, j: (l, 0, j))],
        out_specs=pl.BlockSpec((1, MOD_ROWS, bn), lambda l, j: (l, 0, j)),
        out_shape=jax.ShapeDtypeStruct((depth, MOD_ROWS, n), F32),
        compiler_params=_cparams("parallel", "parallel"),
        name="ada",
    )(s, w_ada, b_ada.reshape(depth, 1, n))


def _bias_table_kernel(rpb_ref, o_ref):
    l, h = pl.program_id(0), pl.program_id(1)
    n_dr, n_dc = 2 * NA_H - 1, 2 * NA_W - 1
    base = (l * B_HEADS + h) * n_dr * n_dc
    qc = lax.broadcasted_iota(jnp.int32, (GRID_W, GRID_W), 0)
    kc = lax.broadcasted_iota(jnp.int32, (GRID_W, GRID_W), 1)
    cs = jnp.clip(qc - NA_W // 2, 0, GRID_W - NA_W)
    col_valid = (kc >= cs) & (kc < cs + NA_W)
    dc = jnp.clip(kc - qc + NA_W - 1, 0, 2 * NA_W - 2)
    neg = jnp.full((GRID_W, GRID_W), NEG, F32)
    toe = []
    for d in range(n_dr):
        t = jnp.zeros((GRID_W, GRID_W), F32)
        for e in range(n_dc):
            t = jnp.where(dc == e, rpb_ref[base + d * n_dc + e] * LOG2E, t)
        toe.append(jnp.where(col_valid, t, neg))
    half = NA_H // 2
    for typ in range(3):
        for a in range(NB_ROWS):
            if typ == 0:
                rs_rel, dr0 = max(a - half, 0), NA_H - 1 - a
            elif typ == 1:
                rs_rel, dr0 = a, NA_H - 1 - half - a
            else:
                rs_rel, dr0 = half + min(a, half), NA_H - 1 - NB_ROWS - a
            for jj in range(NB_KROWS):
                valid = rs_rel <= jj < rs_rel + NA_H
                o_ref[0, 0, typ, a * GRID_W:(a + 1) * GRID_W, jj * GRID_W:(jj + 1) * GRID_W] = (
                    toe[jj + dr0] if valid else neg)


def _bias_table(rpb):
    depth = rpb.shape[0]
    shape = (depth, B_HEADS, 3, NB_ROWS * GRID_W, NB_KROWS * GRID_W)
    return pl.pallas_call(
        _bias_table_kernel,
        grid=(depth, B_HEADS),
        in_specs=[pl.BlockSpec(memory_space=pltpu.SMEM)],
        out_specs=pl.BlockSpec((1, 1) + shape[2:], lambda l, h: (l, h, 0, 0, 0)),
        out_shape=jax.ShapeDtypeStruct(shape, F32),
        compiler_params=_cparams("parallel", "parallel"),
        name="bias_table",
    )(rpb.reshape(-1))


class _Stream:
    def __init__(self, n_rows, seq, mod_row):
        assert n_rows % seq == 0
        self.n_rows, self.seq, self.mod_row = n_rows, seq, mod_row

    def block(self, preferred):
        bm = min(preferred, self.seq)
        assert self.seq % bm == 0
        return bm

    def mod_spec(self, d, chunk, bm):
        return pl.BlockSpec((1, 1, d), lambda i, *_: (self.mod_row(i * bm), 0, chunk))


def _modulated_norm(x, g_ref, sh_ref, sc_ref):
    return (_rms(x, g_ref[...]) * (1 + sc_ref[0]) + sh_ref[0]).astype(BF16)


def _prenorm_kernel(x_ref, g_ref, sh_ref, sc_ref, o_ref):
    o_ref[...] = _modulated_norm(x_ref[...], g_ref, sh_ref, sc_ref)


def _prenorm(st, x, g, mod):
    d = x.shape[1]
    bm = st.block(512)
    row = pl.BlockSpec((bm, d), lambda i: (i, 0))
    return pl.pallas_call(
        _prenorm_kernel,
        grid=(st.n_rows // bm,),
        in_specs=[row, pl.BlockSpec((1, d), lambda i: (0, 0)), st.mod_spec(d, 0, bm), st.mod_spec(d, 1, bm)],
        out_specs=row,
        out_shape=jax.ShapeDtypeStruct(x.shape, BF16),
        compiler_params=_cparams("parallel"),
        name="prenorm",
    )(x, g.reshape(1, d), mod, mod)


def _swap_half_pairs(x):
    lane = lax.broadcasted_iota(jnp.int32, (1, HEAD_DIM), 1)
    first = (lane % (HEAD_DIM // 2)) < HEAD_DIM // 4
    return jnp.where(first, pltpu.roll(x, HEAD_DIM - HEAD_DIM // 4, 1), pltpu.roll(x, HEAD_DIM // 4, 1))


def _inproj_kernel(*refs, rope):
    if rope:
        h_ref, w_ref, qn_ref, kn_ref, cos_ref, sin_ref = refs[:6]
        out_refs = refs[6:]
    else:
        h_ref, w_ref, qn_ref, kn_ref = refs[:4]
        out_refs = refs[4:]
    h = h_ref[...]
    col = 0
    for (is_q, has_norm, has_rope), width, o_ref in zip(QKV_KIND, QKV_SIZES, out_refs):
        for c in range(0, width, 2 * HEAD_DIM):
            y = _dot(h, w_ref[:, col + c:col + c + 2 * HEAD_DIM])
            for s in range(2):
                yh = y[:, s * HEAD_DIM:(s + 1) * HEAD_DIM]
                if has_norm:
                    yh = _rms(yh, (qn_ref if is_q else kn_ref)[...])
                if has_rope and rope:
                    yh = yh * cos_ref[...] + _swap_half_pairs(yh) * sin_ref[...]
                if is_q:
                    yh = yh * QSCALE
                o_ref[:, c + s * HEAD_DIM:c + (s + 1) * HEAD_DIM] = yh.astype(BF16)
        col += width


def _inproj(st, h, w_in, layer, qn, kn, rope_tabs):
    d = h.shape[1]
    bm = st.block(512)
    rope = rope_tabs is not None
    row = lambda width: pl.BlockSpec((bm, width), lambda i: (i, 0))
    vec = pl.BlockSpec((1, HEAD_DIM), lambda i: (0, 0))
    wspec = pl.BlockSpec((None, d, QKV_COLS), lambda i: (layer, 0, 0), pipeline_mode=pl.Buffered(1))
    in_specs = [row(d), wspec, vec, vec]
    args = [h, w_in, qn.reshape(1, HEAD_DIM), kn.reshape(1, HEAD_DIM)]
    if rope:
        per_seq = st.seq // bm
        tab = pl.BlockSpec((bm, HEAD_DIM), lambda i: (i % per_seq, 0))
        in_specs += [tab, tab]
        args += list(rope_tabs)
    return pl.pallas_call(
        functools.partial(_inproj_kernel, rope=rope),
        grid=(st.n_rows // bm,),
        in_specs=in_specs,
        out_specs=[row(width) for width in QKV_SIZES],
        out_shape=[jax.ShapeDtypeStruct((st.n_rows, width), BF16) for width in QKV_SIZES],
        compiler_params=_cparams("parallel"),
        name="inproj",
    )(*args)


def _softmax_pv(s, sx, v, vx, sink=None):
    m = jnp.maximum(jnp.max(s, axis=-1, keepdims=True), jnp.max(sx, axis=-1, keepdims=True))
    if sink is not None:
        m = jnp.maximum(m, sink)
    e = jnp.exp2(s - m)
    ex = jnp.exp2(sx - m)
    den = jnp.sum(e, axis=-1, keepdims=True) + jnp.sum(ex, axis=-1, keepdims=True)
    if sink is not None:
        den = den + jnp.exp2(sink - m)
    o = _dot(e.astype(BF16), v) + _dot(ex.astype(BF16), vx)
    return o / den


def _attn_a_kernel(sink_ref, q_ref, k_ref, v_ref, kx_ref, vx_ref, o_ref, *, bq, seq):
    kv, i = pl.program_id(1), pl.program_id(2)
    wlen = bq + 2 * WINDOW
    q0 = i * bq
    start = pl.multiple_of(jnp.clip(q0 - WINDOW, 0, seq - wlen), WINDOW)
    kw = k_ref[pl.ds(start, wlen), :]
    vw = v_ref[pl.ds(start, wlen), :]
    qpos = q0 + lax.broadcasted_iota(jnp.int32, (bq, 1), 0)
    kpos = start + lax.broadcasted_iota(jnp.int32, (1, wlen), 1)
    valid = jnp.abs(kpos - qpos) <= WINDOW
    kx, vx = kx_ref[...], vx_ref[...]
    group = A_HEADS // A_KV
    for g in range(group):
        q = q_ref[:, g * HEAD_DIM:(g + 1) * HEAD_DIM]
        s = jnp.where(valid, _dot_t(q, kw), NEG)
        o = _softmax_pv(s, _dot_t(q, kx), vw, vx, sink_ref[kv * group + g] * LOG2E)
        o_ref[:, g * HEAD_DIM:(g + 1) * HEAD_DIM] = o.astype(BF16)


def _attn_a(q, k, v, kx, vx, sink, batch, seq, ctx_len):
    bq = 256
    assert seq % bq == 0 and seq >= bq + 2 * WINDOW
    nq = seq // bq
    gw = (A_HEADS // A_KV) * HEAD_DIM
    qspec = pl.BlockSpec((bq, gw), lambda b, kv, i: (b * nq + i, kv))
    kvspec = pl.BlockSpec((seq, HEAD_DIM), lambda b, kv, i: (b, kv))
    xspec = pl.BlockSpec((ctx_len, HEAD_DIM), lambda b, kv, i: (b, kv))
    return pl.pallas_call(
        functools.partial(_attn_a_kernel, bq=bq, seq=seq),
        grid=(batch, A_KV, nq),
        in_specs=[pl.BlockSpec(memory_space=pltpu.SMEM), qspec, kvspec, kvspec, xspec, xspec],
        out_specs=qspec,
        out_shape=jax.ShapeDtypeStruct(q.shape, BF16),
        compiler_params=_cparams("parallel", "parallel", "parallel"),
        name="attn_window",
    )(sink, q, k, v, kx, vx)


def _attn_b_kernel(q_ref, k_ref, v_ref, kx_ref, vx_ref, tab_ref, o_ref, *, rows):
    rb = pl.program_id(2)
    ks = jnp.clip(rb * NB_ROWS - NA_H // 2, 0, rows - NB_KROWS)
    start = pl.multiple_of(ks * GRID_W, GRID_W)
    n_keys = NB_KROWS * GRID_W
    q = q_ref[...]
    s = _dot_t(q, k_ref[pl.ds(start, n_keys), :]) + tab_ref[0, 0]
    o = _softmax_pv(s, _dot_t(q, kx_ref[...]), v_ref[pl.ds(start, n_keys), :], vx_ref[...])
    o_ref[...] = o.astype(BF16)


def _attn_b(q, k, v, kx, vx, table, batch, seq, ctx_len):
    rows = seq // GRID_W
    assert seq % GRID_W == 0 and rows % NB_ROWS == 0 and rows >= NB_KROWS
    nrb = rows // NB_ROWS
    bq = NB_ROWS * GRID_W
    qspec = pl.BlockSpec((bq, HEAD_DIM), lambda h, b, rb: (b * nrb + rb, h))
    kvspec = pl.BlockSpec((seq, HEAD_DIM), lambda h, b, rb: (b, h))
    xspec = pl.BlockSpec((ctx_len, HEAD_DIM), lambda h, b, rb: (b, h))
    tspec = pl.BlockSpec(
        (1, 1, bq, NB_KROWS * GRID_W),
        lambda h, b, rb: (h, jnp.where(rb == 0, 0, jnp.where(rb == nrb - 1, 2, 1)), 0, 0))
    return pl.pallas_call(
        functools.partial(_attn_b_kernel, rows=rows),
        grid=(B_HEADS, batch, nrb),
        in_specs=[qspec, kvspec, kvspec, xspec, xspec, tspec],
        out_specs=qspec,
        out_shape=jax.ShapeDtypeStruct(q.shape, BF16),
        compiler_params=_cparams("parallel", "parallel", "parallel"),
        name="attn_neighbourhood",
    )(q, k, v, kx, vx, table)


def _attn_c_kernel(q_ref, k_ref, v_ref, kx_ref, vx_ref, o_ref):
    k, v, kx, vx = k_ref[...], v_ref[...], kx_ref[...], vx_ref[...]
    for g in range(C_HEADS // C_KV):
        q = q_ref[:, g * HEAD_DIM:(g + 1) * HEAD_DIM]
        o = _softmax_pv(_dot_t(q, k), _dot_t(q, kx), v, vx)
        o_ref[:, g * HEAD_DIM:(g + 1) * HEAD_DIM] = o.astype(BF16)


def _attn_c(q, k, v, kx, vx, batch, seq, ctx_len):
    bq = 256
    assert seq % bq == 0
    nq = seq // bq
    gw = (C_HEADS // C_KV) * HEAD_DIM
    qspec = pl.BlockSpec((bq, gw), lambda b, kv, i: (b * nq + i, kv))
    kvspec = pl.BlockSpec((seq, HEAD_DIM), lambda b, kv, i: (b, kv))
    xspec = pl.BlockSpec((ctx_len, HEAD_DIM), lambda b, kv, i: (b, kv))
    return pl.pallas_call(
        _attn_c_kernel,
        grid=(batch, C_KV, nq),
        in_specs=[qspec, kvspec, kvspec, xspec, xspec],
        out_specs=qspec,
        out_shape=jax.ShapeDtypeStruct(q.shape, BF16),
        compiler_params=_cparams("parallel", "parallel", "parallel"),
        name="attn_global",
    )(q, k, v, kx, vx)


def _softmax_pv_dense(s, v, sink=None):
    m = jnp.max(s, axis=-1, keepdims=True)
    if sink is not None:
        m = jnp.maximum(m, sink)
    e = jnp.exp2(s - m)
    den = jnp.sum(e, axis=-1, keepdims=True)
    if sink is not None:
        den = den + jnp.exp2(sink - m)
    return _dot(e.astype(BF16), v) / den


def _attn_ctx_kernel(sink_ref, qa_ref, ka_ref, va_ref, qb_ref, kb_ref, vb_ref, qc_ref, kc_ref, vc_ref,
                     oa_ref, ob_ref, oc_ref):
    def head(ref, h):
        return ref[:, h * HEAD_DIM:(h + 1) * HEAD_DIM]

    def put(ref, h, o):
        ref[:, h * HEAD_DIM:(h + 1) * HEAD_DIM] = o.astype(BF16)

    ga, gc = A_HEADS // A_KV, C_HEADS // C_KV
    for h in range(A_HEADS):
        s = _dot_t(head(qa_ref, h), head(ka_ref, h // ga))
        put(oa_ref, h, _softmax_pv_dense(s, head(va_ref, h // ga), sink_ref[h] * LOG2E))
    for h in range(B_HEADS):
        put(ob_ref, h, _softmax_pv_dense(_dot_t(head(qb_ref, h), head(kb_ref, h)), head(vb_ref, h)))
    for h in range(C_HEADS):
        s = _dot_t(head(qc_ref, h), head(kc_ref, h // gc))
        put(oc_ref, h, _softmax_pv_dense(s, head(vc_ref, h // gc)))


def _attn_ctx(qkv, sink, batch, ctx_len):
    spec = lambda width: pl.BlockSpec((ctx_len, width), lambda b: (b, 0))
    out_w = (A_WIDTH, B_WIDTH, C_WIDTH)
    return pl.pallas_call(
        _attn_ctx_kernel,
        grid=(batch,),
        in_specs=[pl.BlockSpec(memory_space=pltpu.SMEM)] + [spec(w) for w in QKV_SIZES],
        out_specs=[spec(w) for w in out_w],
        out_shape=[jax.ShapeDtypeStruct((batch * ctx_len, w), BF16) for w in out_w],
        compiler_params=_cparams("parallel"),
        name="attn_ctx",
    )(sink, *qkv)


def _gate_merge_kernel(h_ref, oa_ref, ob_ref, oc_ref, wga_ref, wgb_ref, wgc_ref, wb_ref, m_ref):
    h = h_ref[...]
    m = (jax.nn.sigmoid(_dot(h, wga_ref[...])) * _dot(oa_ref[...], wb_ref[:A_WIDTH, :])
         + jax.nn.sigmoid(_dot(h, wgb_ref[...])) * _dot(ob_ref[...], wb_ref[A_WIDTH:A_WIDTH + B_WIDTH, :])
         + jax.nn.sigmoid(_dot(h, wgc_ref[...])) * _dot(oc_ref[...], wb_ref[A_WIDTH + B_WIDTH:, :]))
    m_ref[...] = m.astype(BF16)


def _gate_merge(st, h, o_a, o_b, o_c, w_in, w_branch, layer):
    d = h.shape[1]
    bm, bj = st.block(512), 256
    nj = d // bj
    gate0 = QKV_COLS // bj
    row = lambda width: pl.BlockSpec((bm, width), lambda i, j: (i, 0))
    gspec = lambda k: pl.BlockSpec((None, d, bj), lambda i, j: (layer, 0, gate0 + k * nj + j))
    return pl.pallas_call(
        _gate_merge_kernel,
        grid=(st.n_rows // bm, nj),
        in_specs=[row(d), row(A_WIDTH), row(B_WIDTH), row(C_WIDTH), gspec(0), gspec(1), gspec(2),
                  pl.BlockSpec((None, w_branch.shape[1], bj), lambda i, j: (layer, 0, j))],
        out_specs=pl.BlockSpec((bm, bj), lambda i, j: (i, j)),
        out_shape=jax.ShapeDtypeStruct(h.shape, BF16),
        compiler_params=_cparams("parallel", "parallel"),
        name="gate_merge",
    )(h, o_a, o_b, o_c, w_in, w_in, w_in, w_branch)


def _resproj_kernel(*refs, last):
    x_ref, a_ref, w_ref, gate_ref, gn_ref = refs[:5]
    xn = x_ref[...] + gate_ref[0] * _dot(a_ref[...], w_ref[...])
    if last:
        refs[5][...] = _rms(xn, gn_ref[...])
    else:
        shn_ref, scn_ref, xo_ref, hn_ref = refs[5:]
        xo_ref[...] = xn
        hn_ref[...] = _modulated_norm(xn, gn_ref, shn_ref, scn_ref)


def _resproj(st, x, a, w, layer, mod, gate_chunk, g_next, mod_next, next_chunks):
    d = x.shape[1]
    k = a.shape[1]
    bm = st.block(256)
    last = mod_next is None
    row = lambda width: pl.BlockSpec((bm, width), lambda i: (i, 0))
    in_specs = [row(d), row(k),
                pl.BlockSpec((None, k, d), lambda i: (layer, 0, 0), pipeline_mode=pl.Buffered(1)),
                st.mod_spec(d, gate_chunk, bm), pl.BlockSpec((1, d), lambda i: (0, 0))]
    args = [x, a, w, mod, g_next.reshape(1, d)]
    if last:
        out_specs = row(d)
        out_shape = jax.ShapeDtypeStruct(x.shape, F32)
    else:
        in_specs += [st.mod_spec(d, next_chunks[0], bm), st.mod_spec(d, next_chunks[1], bm)]
        args += [mod_next, mod_next]
        out_specs = [row(d), row(d)]
        out_shape = [jax.ShapeDtypeStruct(x.shape, F32), jax.ShapeDtypeStruct(x.shape, BF16)]
    return pl.pallas_call(
        functools.partial(_resproj_kernel, last=last),
        grid=(st.n_rows // bm,),
        in_specs=in_specs,
        out_specs=out_specs,
        out_shape=out_shape,
        compiler_params=_cparams("parallel"),
        name="resproj",
    )(*args)


HALO = 16


def _convglu_up_kernel(h_ref, hp_ref, hn_ref, wa_ref, wu_ref, cw_ref, cb_ref, g_ref, hext_ref, *, bm, seq, sub):
    i, j = pl.program_id(0), pl.program_id(1)

    @pl.when(j == 0)
    def _():
        pos = (i * bm) % seq
        zeros = jnp.zeros_like(hp_ref)
        hext_ref[:HALO, :] = jnp.where(pos == 0, zeros, hp_ref[...])
        hext_ref[HALO:HALO + bm, :] = h_ref[...]
        hext_ref[HALO + bm:, :] = jnp.where(pos + bm == seq, zeros, hn_ref[...])

    for c in range(0, g_ref.shape[1], sub):
        a_ext = _dot(hext_ref[...], wa_ref[:, c:c + sub])
        u = _dot(h_ref[...], wu_ref[:, c:c + sub])
        a = (a_ext[HALO - 1:HALO - 1 + bm] * cw_ref[0:1, c:c + sub] + a_ext[HALO:HALO + bm] * cw_ref[1:2, c:c + sub]
             + a_ext[HALO + 1:HALO + 1 + bm] * cw_ref[2:3, c:c + sub] + cb_ref[:, c:c + sub])
        g_ref[:, c:c + sub] = (a * jax.nn.sigmoid(a) * u).astype(BF16)


def _convglu_up(st, h2, w_up, cw, cb, layer):
    d = h2.shape[1]
    ff = w_up.shape[2] // 2
    bm, bn = st.block(512), 512
    assert ff % bn == 0 and bm % HALO == 0
    nj = ff // bn
    per_blk = bm // HALO
    n_halo = st.n_rows // HALO
    return pl.pallas_call(
        functools.partial(_convglu_up_kernel, bm=bm, seq=st.seq, sub=256),
        grid=(st.n_rows // bm, nj),
        in_specs=[pl.BlockSpec((bm, d), lambda i, j: (i, 0)),
                  pl.BlockSpec((HALO, d), lambda i, j: (jnp.maximum(i * per_blk - 1, 0), 0)),
                  pl.BlockSpec((HALO, d), lambda i, j: (jnp.minimum((i + 1) * per_blk, n_halo - 1), 0)),
                  pl.BlockSpec((None, d, bn), lambda i, j: (layer, 0, j)),
                  pl.BlockSpec((None, d, bn), lambda i, j: (layer, 0, nj + j)),
                  pl.BlockSpec((None, CONV_W, bn), lambda i, j: (layer, 0, j)),
                  pl.BlockSpec((None, 1, bn), lambda i, j: (layer, 0, j))],
        out_specs=pl.BlockSpec((bm, bn), lambda i, j: (i, j)),
        out_shape=jax.ShapeDtypeStruct((st.n_rows, ff), BF16),
        scratch_shapes=[pltpu.VMEM((bm + 2 * HALO, d), BF16)],
        compiler_params=_cparams("parallel", "arbitrary"),
        name="convglu_up",
    )(h2, h2, h2, w_up, w_up, cw, cb.reshape(cb.shape[0], 1, ff))


def _rope_tables(seq):
    t = jnp.arange(seq)
    row = (t // GRID_W).astype(F32)
    col = (t % GRID_W).astype(F32)
    quarter = HEAD_DIM // 4
    inv = ROPE_THETA ** (-jnp.arange(quarter, dtype=F32) / quarter)
    ang_r = row[:, None] * inv[None, :]
    ang_c = col[:, None] * inv[None, :]
    cr, sr, cc, sc = jnp.cos(ang_r), jnp.sin(ang_r), jnp.cos(ang_c), jnp.sin(ang_c)
    return (jnp.concatenate([cr, cr, cc, cc], axis=-1), jnp.concatenate([-sr, sr, -sc, sc], axis=-1))


def kernel(x, c, ctx, c_ctx, w_ada, b_ada, norm1, w_in, sink_a, rpb_b, qnorm_c, knorm_c, w_branch, w_out,
           norm2, w_up, conv_w, conv_b, w_down, final_norm):
    batch, seq, d = x.shape
    ctx_len = ctx.shape[1]
    depth = w_ada.shape[0]
    assert batch + 1 <= MOD_ROWS

    lat = _Stream(batch * seq, seq, lambda r: r // seq)
    cst = _Stream(batch * ctx_len, ctx_len, lambda r: batch)

    s = jnp.concatenate([c, c_ctx[None, :], jnp.zeros((MOD_ROWS - batch - 1, d), F32)], axis=0)
    mods = _ada(s, w_ada, b_ada).reshape(depth, MOD_ROWS, 1, 6 * d)
    table = _bias_table(rpb_b)
    rope_tabs = _rope_tables(seq)
    w_in, w_branch, w_out, w_up, w_down = (w.astype(BF16) for w in (w_in, w_branch, w_out, w_up, w_down))

    def mixer_tail(st, xs, hs, o_a, o_b, o_c, l):
        m = _gate_merge(st, hs, o_a, o_b, o_c, w_in, w_branch, l)
        xs, h2 = _resproj(st, xs, m, w_out, l, mods[l], 2, norm2[l], mods[l], (3, 4))
        g = _convglu_up(st, h2, w_up, conv_w, conv_b, l)
        if l == depth - 1:
            return _resproj(st, xs, g, w_down, l, mods[l], 5, final_norm, None, None)
        return _resproj(st, xs, g, w_down, l, mods[l], 5, norm1[l + 1], mods[l + 1], (0, 1))

    xl = x.reshape(batch * seq, d)
    xc = ctx.reshape(batch * ctx_len, d)
    h = _prenorm(lat, xl, norm1[0], mods[0])
    hc = _prenorm(cst, xc, norm1[0], mods[0])
    out = None
    for l in range(depth):
        qa, ka, va, qb, kb, vb, qc, kc, vc = _inproj(lat, h, w_in, l, qnorm_c[l], knorm_c[l], rope_tabs)
        qkv_x = _inproj(cst, hc, w_in, l, qnorm_c[l], knorm_c[l], None)
        _, ka_x, va_x, _, kb_x, vb_x, _, kc_x, vc_x = qkv_x
        o_a = _attn_a(qa, ka, va, ka_x, va_x, sink_a[l], batch, seq, ctx_len)
        o_b = _attn_b(qb, kb, vb, kb_x, vb_x, table[l], batch, seq, ctx_len)
        o_c = _attn_c(qc, kc, vc, kc_x, vc_x, batch, seq, ctx_len)
        if l == depth - 1:
            out = mixer_tail(lat, xl, h, o_a, o_b, o_c, l)
        else:
            xl, h = mixer_tail(lat, xl, h, o_a, o_b, o_c, l)
            oa_x, ob_x, oc_x = _attn_ctx(qkv_x, sink_a[l], batch, ctx_len)
            xc, hc = mixer_tail(cst, xc, hc, oa_x, ob_x, oc_x, l)
    return out.reshape(batch, seq, d)
```

```python
import functools

import jax
import jax.numpy as jnp
from jax import lax
from jax.experimental import pallas as pl
from jax.experimental.pallas import tpu as pltpu

F32 = jnp.float32
BF16 = jnp.bfloat16

GRID_W = 64
HEAD_DIM = 128
A_HEADS, A_KV = 6, 2
B_HEADS = 4
C_HEADS, C_KV = 6, 2
WINDOW = 128
NA_H, NA_W = 8, 16
ROPE_THETA = 10000.0
CONV_W = 3
N_BRANCH = 3
EPS = 1e-6
NEG = -1e30
LOG2E = 1.4426950408889634
QSCALE = HEAD_DIM ** -0.5 * LOG2E
A_WIDTH = A_HEADS * HEAD_DIM
B_WIDTH = B_HEADS * HEAD_DIM
C_WIDTH = C_HEADS * HEAD_DIM
QKV_SIZES = (A_HEADS * HEAD_DIM, A_KV * HEAD_DIM, A_KV * HEAD_DIM,
             B_HEADS * HEAD_DIM, B_HEADS * HEAD_DIM, B_HEADS * HEAD_DIM,
             C_HEADS * HEAD_DIM, C_KV * HEAD_DIM, C_KV * HEAD_DIM)
QKV_COLS = sum(QKV_SIZES)
QKV_KIND = ((True, False, True), (False, False, True), (False, False, False),
            (True, False, False), (False, False, False), (False, False, False),
            (True, True, True), (False, True, True), (False, False, False))
MOD_ROWS = 8
NB_ROWS = 8
NB_KROWS = 16
V7X_VMEM_BYTES = 64 * 1024 * 1024
VMEM_LIMIT = V7X_VMEM_BYTES - 8 * 1024 * 1024


def _cparams(*sem):
    return pltpu.CompilerParams(dimension_semantics=sem, vmem_limit_bytes=VMEM_LIMIT)


def _dot(a, b):
    return jnp.dot(a, b, preferred_element_type=F32)


def _dot_t(a, b):
    return lax.dot_general(a, b, (((1,), (1,)), ((), ())), preferred_element_type=F32)


def _rms(x, g):
    return x * lax.rsqrt(jnp.mean(x * x, axis=-1, keepdims=True) + EPS) * g


def _ada_kernel(s_ref, w_ref, b_ref, o_ref):
    s = s_ref[...]
    s = s * jax.nn.sigmoid(s)
    o_ref[0] = _dot(s.astype(BF16), w_ref[0].astype(BF16)) + b_ref[0]


def _ada(s, w_ada, b_ada):
    depth, d, n = w_ada.shape
    bn = 1024
    return pl.pallas_call(
        _ada_kernel,
        grid=(depth, n // bn),
        in_specs=[pl.BlockSpec((MOD_ROWS, d), lambda l, j: (0, 0)),
                  pl.BlockSpec((1, d, bn), lambda l, j: (l, 0, j)),
                  pl.BlockSpec((1, 1, bn), lambda l, j: (l, 0, j))],
        out_specs=pl.BlockSpec((1, MOD_ROWS, bn), lambda l, j: (l, 0, j)),
        out_shape=jax.ShapeDtypeStruct((depth, MOD_ROWS, n), F32),
        compiler_params=_cparams("parallel", "parallel"),
        name="ada",
    )(s, w_ada, b_ada.reshape(depth, 1, n))


def _bias_table_kernel(rpb_ref, o_ref):
    l, h = pl.program_id(0), pl.program_id(1)
    n_dr, n_dc = 2 * NA_H - 1, 2 * NA_W - 1
    base = (l * B_HEADS + h) * n_dr * n_dc
    qc = lax.broadcasted_iota(jnp.int32, (GRID_W, GRID_W), 0)
    kc = lax.broadcasted_iota(jnp.int32, (GRID_W, GRID_W), 1)
    cs = jnp.clip(qc - NA_W // 2, 0, GRID_W - NA_W)
    col_valid = (kc >= cs) & (kc < cs + NA_W)
    dc = jnp.clip(kc - qc + NA_W - 1, 0, 2 * NA_W - 2)
    neg = jnp.full((GRID_W, GRID_W), NEG, F32)
    toe = []
    for d in range(n_dr):
        t = jnp.zeros((GRID_W, GRID_W), F32)
        for e in range(n_dc):
            t = jnp.where(dc == e, rpb_ref[base + d * n_dc + e] * LOG2E, t)
        toe.append(jnp.where(col_valid, t, neg))
    half = NA_H // 2
    for typ in range(3):
        for a in range(NB_ROWS):
            if typ == 0:
                rs_rel, dr0 = max(a - half, 0), NA_H - 1 - a
            elif typ == 1:
                rs_rel, dr0 = a, NA_H - 1 - half - a
            else:
                rs_rel, dr0 = half + min(a, half), NA_H - 1 - NB_ROWS - a
            for jj in range(NB_KROWS):
                valid = rs_rel <= jj < rs_rel + NA_H
                o_ref[0, 0, typ, a * GRID_W:(a + 1) * GRID_W, jj * GRID_W:(jj + 1) * GRID_W] = (
                    toe[jj + dr0] if valid else neg)


def _bias_table(rpb):
    depth = rpb.shape[0]
    shape = (depth, B_HEADS, 3, NB_ROWS * GRID_W, NB_KROWS * GRID_W)
    return pl.pallas_call(
        _bias_table_kernel,
        grid=(depth, B_HEADS),
        in_specs=[pl.BlockSpec(memory_space=pltpu.SMEM)],
        out_specs=pl.BlockSpec((1, 1) + shape[2:], lambda l, h: (l, h, 0, 0, 0)),
        out_shape=jax.ShapeDtypeStruct(shape, F32),
        compiler_params=_cparams("parallel", "parallel"),
        name="bias_table",
    )(rpb.reshape(-1))


class _Stream:
    def __init__(self, n_rows, seq, mod_row):
        assert n_rows % seq == 0
        self.n_rows, self.seq, self.mod_row = n_rows, seq, mod_row

    def block(self, preferred):
        bm = min(preferred, self.seq)
        assert self.seq % bm == 0
        return bm

    def mod_spec(self, d, chunk, bm):
        return pl.BlockSpec((1, 1, d), lambda i, *_: (self.mod_row(i * bm), 0, chunk))


def _modulated_norm(x, g_ref, sh_ref, sc_ref):
    return (_rms(x, g_ref[...]) * (1 + sc_ref[0]) + sh_ref[0]).astype(BF16)


def _prenorm_kernel(x_ref, g_ref, sh_ref, sc_ref, o_ref):
    o_ref[...] = _modulated_norm(x_ref[...], g_ref, sh_ref, sc_ref)


def _prenorm(st, x, g, mod):
    d = x.shape[1]
    bm = st.block(512)
    row = pl.BlockSpec((bm, d), lambda i: (i, 0))
    return pl.pallas_call(
        _prenorm_kernel,
        grid=(st.n_rows // bm,),
        in_specs=[row, pl.BlockSpec((1, d), lambda i: (0, 0)), st.mod_spec(d, 0, bm), st.mod_spec(d, 1, bm)],
        out_specs=row,
        out_shape=jax.ShapeDtypeStruct(x.shape, BF16),
        compiler_params=_cparams("parallel"),
        name="prenorm",
    )(x, g.reshape(1, d), mod, mod)


def _swap_half_pairs(x):
    lane = lax.broadcasted_iota(jnp.int32, (1, HEAD_DIM), 1)
    first = (lane % (HEAD_DIM // 2)) < HEAD_DIM // 4
    return jnp.where(first, pltpu.roll(x, HEAD_DIM - HEAD_DIM // 4, 1), pltpu.roll(x, HEAD_DIM // 4, 1))


def _inproj_kernel(*refs, rope):
    if rope:
        h_ref, w_ref, qn_ref, kn_ref, cos_ref, sin_ref = refs[:6]
        out_refs = refs[6:]
    else:
        h_ref, w_ref, qn_ref, kn_ref = refs[:4]
        out_refs = refs[4:]
    h = h_ref[...]
    col = 0
    for (is_q, has_norm, has_rope), width, o_ref in zip(QKV_KIND, QKV_SIZES, out_refs):
        for c in range(0, width, 2 * HEAD_DIM):
            y = _dot(h, w_ref[:, col + c:col + c + 2 * HEAD_DIM])
            for s in range(2):
                yh = y[:, s * HEAD_DIM:(s + 1) * HEAD_DIM]
                if has_norm:
                    yh = _rms(yh, (qn_ref if is_q else kn_ref)[...])
                if has_rope and rope:
                    yh = yh * cos_ref[...] + _swap_half_pairs(yh) * sin_ref[...]
                if is_q:
                    yh = yh * QSCALE
                o_ref[:, c + s * HEAD_DIM:c + (s + 1) * HEAD_DIM] = yh.astype(BF16)
        col += width


def _inproj(st, h, w_in, layer, qn, kn, rope_tabs):
    d = h.shape[1]
    bm = st.block(512)
    rope = rope_tabs is not None
    row = lambda width: pl.BlockSpec((bm, width), lambda i: (i, 0))
    vec = pl.BlockSpec((1, HEAD_DIM), lambda i: (0, 0))
    wspec = pl.BlockSpec((None, d, QKV_COLS), lambda i: (layer, 0, 0), pipeline_mode=pl.Buffered(1))
    in_specs = [row(d), wspec, vec, vec]
    args = [h, w_in, qn.reshape(1, HEAD_DIM), kn.reshape(1, HEAD_DIM)]
    if rope:
        per_seq = st.seq // bm
        tab = pl.BlockSpec((bm, HEAD_DIM), lambda i: (i % per_seq, 0))
        in_specs += [tab, tab]
        args += list(rope_tabs)
    return pl.pallas_call(
        functools.partial(_inproj_kernel, rope=rope),
        grid=(st.n_rows // bm,),
        in_specs=in_specs,
        out_specs=[row(width) for width in QKV_SIZES],
        out_shape=[jax.ShapeDtypeStruct((st.n_rows, width), BF16) for width in QKV_SIZES],
        compiler_params=_cparams("parallel"),
        name="inproj",
    )(*args)


def _softmax_pv(s, sx, v, vx, sink=None):
    m = jnp.maximum(jnp.max(s, axis=-1, keepdims=True), jnp.max(sx, axis=-1, keepdims=True))
    if sink is not None:
        m = jnp.maximum(m, sink)
    e = jnp.exp2(s - m)
    ex = jnp.exp2(sx - m)
    den = jnp.sum(e, axis=-1, keepdims=True) + jnp.sum(ex, axis=-1, keepdims=True)
    if sink is not None:
        den = den + jnp.exp2(sink - m)
    o = _dot(e.astype(BF16), v) + _dot(ex.astype(BF16), vx)
    return o / den


def _attn_a_kernel(sink_ref, q_ref, k_ref, v_ref, kx_ref, vx_ref, o_ref, *, bq, seq):
    kv, i = pl.program_id(1), pl.program_id(2)
    wlen = bq + 2 * WINDOW
    q0 = i * bq
    start = pl.multiple_of(jnp.clip(q0 - WINDOW, 0, seq - wlen), WINDOW)
    kw = k_ref[pl.ds(start, wlen), :]
    vw = v_ref[pl.ds(start, wlen), :]
    qpos = q0 + lax.broadcasted_iota(jnp.int32, (bq, 1), 0)
    kpos = start + lax.broadcasted_iota(jnp.int32, (1, wlen), 1)
    valid = jnp.abs(kpos - qpos) <= WINDOW
    kx, vx = kx_ref[...], vx_ref[...]
    group = A_HEADS // A_KV

    def scores(g):
        q = q_ref[:, g * HEAD_DIM:(g + 1) * HEAD_DIM]
        return _dot_t(q, kw), _dot_t(q, kx)

    nxt = scores(0)
    for g in range(group):
        s, sx = nxt
        if g + 1 < group:
            nxt = scores(g + 1)
        o = _softmax_pv(jnp.where(valid, s, NEG), sx, vw, vx, sink_ref[kv * group + g] * LOG2E)
        o_ref[:, g * HEAD_DIM:(g + 1) * HEAD_DIM] = o.astype(BF16)


def _attn_a(q, k, v, kx, vx, sink, batch, seq, ctx_len):
    bq = 256
    assert seq % bq == 0 and seq >= bq + 2 * WINDOW
    nq = seq // bq
    gw = (A_HEADS // A_KV) * HEAD_DIM
    qspec = pl.BlockSpec((bq, gw), lambda b, kv, i: (b * nq + i, kv))
    kvspec = pl.BlockSpec((seq, HEAD_DIM), lambda b, kv, i: (b, kv))
    xspec = pl.BlockSpec((ctx_len, HEAD_DIM), lambda b, kv, i: (b, kv))
    return pl.pallas_call(
        functools.partial(_attn_a_kernel, bq=bq, seq=seq),
        grid=(batch, A_KV, nq),
        in_specs=[pl.BlockSpec(memory_space=pltpu.SMEM), qspec, kvspec, kvspec, xspec, xspec],
        out_specs=qspec,
        out_shape=jax.ShapeDtypeStruct(q.shape, BF16),
        compiler_params=_cparams("parallel", "parallel", "parallel"),
        name="attn_window",
    )(sink, q, k, v, kx, vx)


def _attn_b_kernel(q_ref, k_ref, v_ref, kx_ref, vx_ref, tab_ref, o_ref, *, rows, rsub):
    rb = pl.program_id(2)
    ks = jnp.clip(rb * NB_ROWS - NA_H // 2, 0, rows - NB_KROWS)
    start = pl.multiple_of(ks * GRID_W, GRID_W)
    n_keys = NB_KROWS * GRID_W
    kw, vw = k_ref[pl.ds(start, n_keys), :], v_ref[pl.ds(start, n_keys), :]
    kx, vx = kx_ref[...], vx_ref[...]
    bq = q_ref.shape[0]

    def scores(r):
        q = q_ref[r:r + rsub, :]
        return _dot_t(q, kw), _dot_t(q, kx)

    nxt = scores(0)
    for r in range(0, bq, rsub):
        s, sx = nxt
        if r + rsub < bq:
            nxt = scores(r + rsub)
        o = _softmax_pv(s + tab_ref[0, 0, r:r + rsub, :], sx, vw, vx)
        o_ref[r:r + rsub, :] = o.astype(BF16)


def _attn_b(q, k, v, kx, vx, table, batch, seq, ctx_len):
    rows = seq // GRID_W
    assert seq % GRID_W == 0 and rows % NB_ROWS == 0 and rows >= NB_KROWS
    nrb = rows // NB_ROWS
    bq = NB_ROWS * GRID_W
    qspec = pl.BlockSpec((bq, HEAD_DIM), lambda h, b, rb: (b * nrb + rb, h))
    kvspec = pl.BlockSpec((seq, HEAD_DIM), lambda h, b, rb: (b, h))
    xspec = pl.BlockSpec((ctx_len, HEAD_DIM), lambda h, b, rb: (b, h))
    tspec = pl.BlockSpec(
        (1, 1, bq, NB_KROWS * GRID_W),
        lambda h, b, rb: (h, jnp.where(rb == 0, 0, jnp.where(rb == nrb - 1, 2, 1)), 0, 0))
    return pl.pallas_call(
        functools.partial(_attn_b_kernel, rows=rows, rsub=128),
        grid=(B_HEADS, batch, nrb),
        in_specs=[qspec, kvspec, kvspec, xspec, xspec, tspec],
        out_specs=qspec,
        out_shape=jax.ShapeDtypeStruct(q.shape, BF16),
        compiler_params=_cparams("parallel", "parallel", "parallel"),
        name="attn_neighbourhood",
    )(q, k, v, kx, vx, table)


def _attn_c_kernel(q_ref, k_ref, v_ref, kx_ref, vx_ref, o_ref, *, chunk):
    n_lat = k_ref.shape[0] // chunk

    def keys(c):
        return kx_ref[...] if c == 0 else k_ref[(c - 1) * chunk:c * chunk, :]

    def values(c):
        return vx_ref[...] if c == 0 else v_ref[(c - 1) * chunk:c * chunk, :]

    def scores(unit):
        g, c = unit
        return _dot_t(q_ref[:, g * HEAD_DIM:(g + 1) * HEAD_DIM], keys(c))

    units = [(g, c) for g in range(C_HEADS // C_KV) for c in range(n_lat + 1)]
    s_next = scores(units[0])
    m = l = acc = None
    for idx, (g, c) in enumerate(units):
        s = s_next
        if idx + 1 < len(units):
            s_next = scores(units[idx + 1])
        m_blk = jnp.max(s, axis=-1, keepdims=True)
        if c == 0:
            m = m_blk
            p = jnp.exp2(s - m)
            l = jnp.sum(p, axis=-1, keepdims=True)
            acc = _dot(p.astype(BF16), values(c))
        else:
            m_new = jnp.maximum(m, m_blk)
            alpha = jnp.exp2(m - m_new)
            p = jnp.exp2(s - m_new)
            l = alpha * l + jnp.sum(p, axis=-1, keepdims=True)
            acc = alpha * acc + _dot(p.astype(BF16), values(c))
            m = m_new
        if c == n_lat:
            o_ref[:, g * HEAD_DIM:(g + 1) * HEAD_DIM] = (acc / l).astype(BF16)


def _attn_c(q, k, v, kx, vx, batch, seq, ctx_len):
    bq = 256
    assert seq % bq == 0
    nq = seq // bq
    gw = (C_HEADS // C_KV) * HEAD_DIM
    qspec = pl.BlockSpec((bq, gw), lambda b, kv, i: (b * nq + i, kv))
    kvspec = pl.BlockSpec((seq, HEAD_DIM), lambda b, kv, i: (b, kv))
    xspec = pl.BlockSpec((ctx_len, HEAD_DIM), lambda b, kv, i: (b, kv))
    return pl.pallas_call(
        functools.partial(_attn_c_kernel, chunk=1024),
        grid=(batch, C_KV, nq),
        in_specs=[qspec, kvspec, kvspec, xspec, xspec],
        out_specs=qspec,
        out_shape=jax.ShapeDtypeStruct(q.shape, BF16),
        compiler_params=_cparams("parallel", "parallel", "parallel"),
        name="attn_global",
    )(q, k, v, kx, vx)


def _softmax_pv_dense(s, v, sink=None):
    m = jnp.max(s, axis=-1, keepdims=True)
    if sink is not None:
        m = jnp.maximum(m, sink)
    e = jnp.exp2(s - m)
    den = jnp.sum(e, axis=-1, keepdims=True)
    if sink is not None:
        den = den + jnp.exp2(sink - m)
    return _dot(e.astype(BF16), v) / den


def _attn_ctx_kernel(sink_ref, qa_ref, ka_ref, va_ref, qb_ref, kb_ref, vb_ref, qc_ref, kc_ref, vc_ref,
                     oa_ref, ob_ref, oc_ref):
    def head(ref, h):
        return ref[:, h * HEAD_DIM:(h + 1) * HEAD_DIM]

    def put(ref, h, o):
        ref[:, h * HEAD_DIM:(h + 1) * HEAD_DIM] = o.astype(BF16)

    ga, gc = A_HEADS // A_KV, C_HEADS // C_KV
    for h in range(A_HEADS):
        s = _dot_t(head(qa_ref, h), head(ka_ref, h // ga))
        put(oa_ref, h, _softmax_pv_dense(s, head(va_ref, h // ga), sink_ref[h] * LOG2E))
    for h in range(B_HEADS):
        put(ob_ref, h, _softmax_pv_dense(_dot_t(head(qb_ref, h), head(kb_ref, h)), head(vb_ref, h)))
    for h in range(C_HEADS):
        s = _dot_t(head(qc_ref, h), head(kc_ref, h // gc))
        put(oc_ref, h, _softmax_pv_dense(s, head(vc_ref, h // gc)))


def _attn_ctx(qkv, sink, batch, ctx_len):
    spec = lambda width: pl.BlockSpec((ctx_len, width), lambda b: (b, 0))
    out_w = (A_WIDTH, B_WIDTH, C_WIDTH)
    return pl.pallas_call(
        _attn_ctx_kernel,
        grid=(batch,),
        in_specs=[pl.BlockSpec(memory_space=pltpu.SMEM)] + [spec(w) for w in QKV_SIZES],
        out_specs=[spec(w) for w in out_w],
        out_shape=[jax.ShapeDtypeStruct((batch * ctx_len, w), BF16) for w in out_w],
        compiler_params=_cparams("parallel"),
        name="attn_ctx",
    )(sink, *qkv)


def _gate_merge_kernel(h_ref, oa_ref, ob_ref, oc_ref, wga_ref, wgb_ref, wgc_ref, wb_ref, m_ref, *, rsub, csub):
    bm, bj = m_ref.shape
    branches = ((oa_ref, wga_ref, 0, A_WIDTH), (ob_ref, wgb_ref, A_WIDTH, A_WIDTH + B_WIDTH),
                (oc_ref, wgc_ref, A_WIDTH + B_WIDTH, A_WIDTH + B_WIDTH + C_WIDTH))

    def dots(unit):
        r, c = unit
        h = h_ref[r:r + rsub, :]
        return [(_dot(h, wg_ref[:, c:c + csub]), _dot(o_ref[r:r + rsub, :], wb_ref[lo:hi, c:c + csub]))
                for o_ref, wg_ref, lo, hi in branches]

    units = [(r, c) for c in range(0, bj, csub) for r in range(0, bm, rsub)]
    nxt = dots(units[0])
    for idx, (r, c) in enumerate(units):
        cur = nxt
        if idx + 1 < len(units):
            nxt = dots(units[idx + 1])
        (ga, ba), (gb, bb), (gc, bc) = cur
        m = jax.nn.sigmoid(ga) * ba + jax.nn.sigmoid(gb) * bb + jax.nn.sigmoid(gc) * bc
        m_ref[r:r + rsub, c:c + csub] = m.astype(BF16)


def _gate_merge(st, h, o_a, o_b, o_c, w_in, w_branch, layer):
    d = h.shape[1]
    bm, bj = st.block(1024), 512
    nj = d // bj
    gate0 = QKV_COLS // bj
    row = lambda width: pl.BlockSpec((bm, width), lambda i, j: (i, 0))
    gspec = lambda k: pl.BlockSpec((None, d, bj), lambda i, j: (layer, 0, gate0 + k * nj + j))
    return pl.pallas_call(
        functools.partial(_gate_merge_kernel, rsub=min(bm, 512), csub=256),
        grid=(st.n_rows // bm, nj),
        in_specs=[row(d), row(A_WIDTH), row(B_WIDTH), row(C_WIDTH), gspec(0), gspec(1), gspec(2),
                  pl.BlockSpec((None, w_branch.shape[1], bj), lambda i, j: (layer, 0, j))],
        out_specs=pl.BlockSpec((bm, bj), lambda i, j: (i, j)),
        out_shape=jax.ShapeDtypeStruct(h.shape, BF16),
        compiler_params=_cparams("parallel", "parallel"),
        name="gate_merge",
    )(h, o_a, o_b, o_c, w_in, w_in, w_in, w_branch)


def _resproj_kernel(*refs, last):
    x_ref, a_ref, w_ref, gate_ref, gn_ref = refs[:5]
    xn = x_ref[...] + gate_ref[0] * _dot(a_ref[...], w_ref[...])
    if last:
        refs[5][...] = _rms(xn, gn_ref[...])
    else:
        shn_ref, scn_ref, xo_ref, hn_ref = refs[5:]
        xo_ref[...] = xn
        hn_ref[...] = _modulated_norm(xn, gn_ref, shn_ref, scn_ref)


def _resproj(st, x, a, w, layer, mod, gate_chunk, g_next, mod_next, next_chunks):
    d = x.shape[1]
    k = a.shape[1]
    bm = st.block(256)
    last = mod_next is None
    row = lambda width: pl.BlockSpec((bm, width), lambda i: (i, 0))
    in_specs = [row(d), row(k),
                pl.BlockSpec((None, k, d), lambda i: (layer, 0, 0), pipeline_mode=pl.Buffered(1)),
                st.mod_spec(d, gate_chunk, bm), pl.BlockSpec((1, d), lambda i: (0, 0))]
    args = [x, a, w, mod, g_next.reshape(1, d)]
    if last:
        out_specs = row(d)
        out_shape = jax.ShapeDtypeStruct(x.shape, F32)
    else:
        in_specs += [st.mod_spec(d, next_chunks[0], bm), st.mod_spec(d, next_chunks[1], bm)]
        args += [mod_next, mod_next]
        out_specs = [row(d), row(d)]
        out_shape = [jax.ShapeDtypeStruct(x.shape, F32), jax.ShapeDtypeStruct(x.shape, BF16)]
    return pl.pallas_call(
        functools.partial(_resproj_kernel, last=last),
        grid=(st.n_rows // bm,),
        in_specs=in_specs,
        out_specs=out_specs,
        out_shape=out_shape,
        compiler_params=_cparams("parallel"),
        name="resproj",
    )(*args)


HALO = 16


def _convglu_up_kernel(h_ref, hp_ref, hn_ref, wa_ref, wu_ref, cw_ref, cb_ref, g_ref, hext_ref, *, seq, rsub, csub):
    i, j = pl.program_id(0), pl.program_id(1)
    bm, bn = g_ref.shape

    @pl.when(j == 0)
    def _():
        pos = (i * bm) % seq
        zeros = jnp.zeros_like(hp_ref)
        hext_ref[:HALO, :] = jnp.where(pos == 0, zeros, hp_ref[...])
        hext_ref[HALO:HALO + bm, :] = h_ref[...]
        hext_ref[HALO + bm:, :] = jnp.where(pos + bm == seq, zeros, hn_ref[...])

    n_ext = rsub + 2 * HALO

    def dots(unit):
        r, c = unit
        return (_dot(hext_ref[r:r + n_ext, :], wa_ref[:, c:c + csub]), _dot(h_ref[r:r + rsub, :], wu_ref[:, c:c + csub]))

    units = [(r, c) for c in range(0, bn, csub) for r in range(0, bm, rsub)]
    nxt = dots(units[0])
    for idx, (r, c) in enumerate(units):
        a_ext, u = nxt
        if idx + 1 < len(units):
            nxt = dots(units[idx + 1])
        a = (pltpu.roll(a_ext, 1, 0) * cw_ref[0:1, c:c + csub] + a_ext * cw_ref[1:2, c:c + csub]
             + pltpu.roll(a_ext, n_ext - 1, 0) * cw_ref[2:3, c:c + csub])[HALO:HALO + rsub] + cb_ref[:, c:c + csub]
        g_ref[r:r + rsub, c:c + csub] = (a * jax.nn.sigmoid(a) * u).astype(BF16)


def _convglu_up(st, h2, w_up, cw, cb, layer):
    d = h2.shape[1]
    ff = w_up.shape[2] // 2
    bm, bn = st.block(1024), 512
    rsub, csub = min(bm, 512), 256
    assert ff % bn == 0 and bm % HALO == 0
    nj = ff // bn
    per_blk = bm // HALO
    n_halo = st.n_rows // HALO
    return pl.pallas_call(
        functools.partial(_convglu_up_kernel, seq=st.seq, rsub=rsub, csub=csub),
        grid=(st.n_rows // bm, nj),
        in_specs=[pl.BlockSpec((bm, d), lambda i, j: (i, 0)),
                  pl.BlockSpec((HALO, d), lambda i, j: (jnp.maximum(i * per_blk - 1, 0), 0)),
                  pl.BlockSpec((HALO, d), lambda i, j: (jnp.minimum((i + 1) * per_blk, n_halo - 1), 0)),
                  pl.BlockSpec((None, d, bn), lambda i, j: (layer, 0, j)),
                  pl.BlockSpec((None, d, bn), lambda i, j: (layer, 0, nj + j)),
                  pl.BlockSpec((None, CONV_W, bn), lambda i, j: (layer, 0, j)),
                  pl.BlockSpec((None, 1, bn), lambda i, j: (layer, 0, j))],
        out_specs=pl.BlockSpec((bm, bn), lambda i, j: (i, j)),
        out_shape=jax.ShapeDtypeStruct((st.n_rows, ff), BF16),
        scratch_shapes=[pltpu.VMEM((bm + 2 * HALO, d), BF16)],
        compiler_params=_cparams("parallel", "arbitrary"),
        name="convglu_up",
    )(h2, h2, h2, w_up, w_up, cw, cb.reshape(cb.shape[0], 1, ff))


def _rope_tables(seq):
    t = jnp.arange(seq)
    row = (t // GRID_W).astype(F32)
    col = (t % GRID_W).astype(F32)
    quarter = HEAD_DIM // 4
    inv = ROPE_THETA ** (-jnp.arange(quarter, dtype=F32) / quarter)
    ang_r = row[:, None] * inv[None, :]
    ang_c = col[:, None] * inv[None, :]
    cr, sr, cc, sc = jnp.cos(ang_r), jnp.sin(ang_r), jnp.cos(ang_c), jnp.sin(ang_c)
    return (jnp.concatenate([cr, cr, cc, cc], axis=-1), jnp.concatenate([-sr, sr, -sc, sc], axis=-1))


def kernel(x, c, ctx, c_ctx, w_ada, b_ada, norm1, w_in, sink_a, rpb_b, qnorm_c, knorm_c, w_branch, w_out,
           norm2, w_up, conv_w, conv_b, w_down, final_norm):
    batch, seq, d = x.shape
    ctx_len = ctx.shape[1]
    depth = w_ada.shape[0]
    assert batch + 1 <= MOD_ROWS

    lat = _Stream(batch * seq, seq, lambda r: r // seq)
    cst = _Stream(batch * ctx_len, ctx_len, lambda r: batch)

    s = jnp.concatenate([c, c_ctx[None, :], jnp.zeros((MOD_ROWS - batch - 1, d), F32)], axis=0)
    mods = _ada(s, w_ada, b_ada).reshape(depth, MOD_ROWS, 1, 6 * d)
    table = _bias_table(rpb_b)
    rope_tabs = _rope_tables(seq)
    w_in, w_branch, w_out, w_up, w_down = (w.astype(BF16) for w in (w_in, w_branch, w_out, w_up, w_down))

    def mixer_tail(st, xs, hs, o_a, o_b, o_c, l):
        m = _gate_merge(st, hs, o_a, o_b, o_c, w_in, w_branch, l)
        xs, h2 = _resproj(st, xs, m, w_out, l, mods[l], 2, norm2[l], mods[l], (3, 4))
        g = _convglu_up(st, h2, w_up, conv_w, conv_b, l)
        if l == depth - 1:
            return _resproj(st, xs, g, w_down, l, mods[l], 5, final_norm, None, None)
        return _resproj(st, xs, g, w_down, l, mods[l], 5, norm1[l + 1], mods[l + 1], (0, 1))

    xl = x.reshape(batch * seq, d)
    xc = ctx.reshape(batch * ctx_len, d)
    h = _prenorm(lat, xl, norm1[0], mods[0])
    hc = _prenorm(cst, xc, norm1[0], mods[0])
    out = None
    for l in range(depth):
        qa, ka, va, qb, kb, vb, qc, kc, vc = _inproj(lat, h, w_in, l, qnorm_c[l], knorm_c[l], rope_tabs)
        qkv_x = _inproj(cst, hc, w_in, l, qnorm_c[l], knorm_c[l], None)
        _, ka_x, va_x, _, kb_x, vb_x, _, kc_x, vc_x = qkv_x
        o_a = _attn_a(qa, ka, va, ka_x, va_x, sink_a[l], batch, seq, ctx_len)
        o_b = _attn_b(qb, kb, vb, kb_x, vb_x, table[l], batch, seq, ctx_len)
        o_c = _attn_c(qc, kc, vc, kc_x, vc_x, batch, seq, ctx_len)
        if l == depth - 1:
            out = mixer_tail(lat, xl, h, o_a, o_b, o_c, l)
        else:
            xl, h = mixer_tail(lat, xl, h, o_a, o_b, o_c, l)
            oa_x, ob_x, oc_x = _attn_ctx(qkv_x, sink_a[l], batch, ctx_len)
            xc, hc = mixer_tail(cst, xc, hc, oa_x, ob_x, oc_x, l)
    return out.reshape(batch, seq, d)
```

```python
import functools

import jax
import jax.numpy as jnp
from jax import lax
from jax.experimental import pallas as pl
from jax.experimental.pallas import tpu as pltpu

F32 = jnp.float32
BF16 = jnp.bfloat16

GRID_W = 64
HEAD_DIM = 128
A_HEADS, A_KV = 6, 2
B_HEADS = 4
C_HEADS, C_KV = 6, 2
WINDOW = 128
NA_H, NA_W = 8, 16
ROPE_THETA = 10000.0
CONV_W = 3
N_BRANCH = 3
EPS = 1e-6
NEG = -1e30
LOG2E = 1.4426950408889634
QSCALE = HEAD_DIM ** -0.5 * LOG2E
A_WIDTH = A_HEADS * HEAD_DIM
B_WIDTH = B_HEADS * HEAD_DIM
C_WIDTH = C_HEADS * HEAD_DIM
QKV_SIZES = (A_HEADS * HEAD_DIM, A_KV * HEAD_DIM, A_KV * HEAD_DIM,
             B_HEADS * HEAD_DIM, B_HEADS * HEAD_DIM, B_HEADS * HEAD_DIM,
             C_HEADS * HEAD_DIM, C_KV * HEAD_DIM, C_KV * HEAD_DIM)
QKV_COLS = sum(QKV_SIZES)
QKV_KIND = ((True, False, True), (False, False, True), (False, False, False),
            (True, False, False), (False, False, False), (False, False, False),
            (True, True, True), (False, True, True), (False, False, False))
MOD_ROWS = 8
NB_ROWS = 8
NB_KROWS = 16
V7X_VMEM_BYTES = 64 * 1024 * 1024
VMEM_LIMIT = V7X_VMEM_BYTES - 8 * 1024 * 1024


def _cparams(*sem):
    return pltpu.CompilerParams(dimension_semantics=sem, vmem_limit_bytes=VMEM_LIMIT)


def _dot(a, b):
    return jnp.dot(a, b, preferred_element_type=F32)


def _dot_t(a, b):
    return lax.dot_general(a, b, (((1,), (1,)), ((), ())), preferred_element_type=F32)


def _rms(x, g):
    return x * lax.rsqrt(jnp.mean(x * x, axis=-1, keepdims=True) + EPS) * g


def _ada_kernel(s_ref, w_ref, b_ref, o_ref):
    s = s_ref[...]
    s = s * jax.nn.sigmoid(s)
    o_ref[0] = _dot(s.astype(BF16), w_ref[0].astype(BF16)) + b_ref[0]


def _ada(s, w_ada, b_ada):
    depth, d, n = w_ada.shape
    bn = 1024
    return pl.pallas_call(
        _ada_kernel,
        grid=(depth, n // bn),
        in_specs=[pl.BlockSpec((MOD_ROWS, d), lambda l, j: (0, 0)),
                  pl.BlockSpec((1, d, bn), lambda l, j: (l, 0, j)),
                  pl.BlockSpec((1, 1, bn), lambda l, j: (l, 0, j))],
        out_specs=pl.BlockSpec((1, MOD_ROWS, bn), lambda l, j: (l, 0, j)),
        out_shape=jax.ShapeDtypeStruct((depth, MOD_ROWS, n), F32),
        compiler_params=_cparams("parallel", "parallel"),
        name="ada",
    )(s, w_ada, b_ada.reshape(depth, 1, n))


def _bias_table_kernel(rpb_ref, o_ref):
    l, h = pl.program_id(0), pl.program_id(1)
    n_dr, n_dc = 2 * NA_H - 1, 2 * NA_W - 1
    base = (l * B_HEADS + h) * n_dr * n_dc
    qc = lax.broadcasted_iota(jnp.int32, (GRID_W, GRID_W), 0)
    kc = lax.broadcasted_iota(jnp.int32, (GRID_W, GRID_W), 1)
    cs = jnp.clip(qc - NA_W // 2, 0, GRID_W - NA_W)
    col_valid = (kc >= cs) & (kc < cs + NA_W)
    dc = jnp.clip(kc - qc + NA_W - 1, 0, 2 * NA_W - 2)
    neg = jnp.full((GRID_W, GRID_W), NEG, F32)
    toe = []
    for d in range(n_dr):
        t = jnp.zeros((GRID_W, GRID_W), F32)
        for e in range(n_dc):
            t = jnp.where(dc == e, rpb_ref[base + d * n_dc + e] * LOG2E, t)
        toe.append(jnp.where(col_valid, t, neg))
    half = NA_H // 2
    for typ in range(3):
        for a in range(NB_ROWS):
            if typ == 0:
                rs_rel, dr0 = max(a - half, 0), NA_H - 1 - a
            elif typ == 1:
                rs_rel, dr0 = a, NA_H - 1 - half - a
            else:
                rs_rel, dr0 = half + min(a, half), NA_H - 1 - NB_ROWS - a
            for jj in range(NB_KROWS):
                valid = rs_rel <= jj < rs_rel + NA_H
                o_ref[0, 0, typ, a * GRID_W:(a + 1) * GRID_W, jj * GRID_W:(jj + 1) * GRID_W] = (
                    toe[jj + dr0] if valid else neg)


def _bias_table(rpb):
    depth = rpb.shape[0]
    shape = (depth, B_HEADS, 3, NB_ROWS * GRID_W, NB_KROWS * GRID_W)
    return pl.pallas_call(
        _bias_table_kernel,
        grid=(depth, B_HEADS),
        in_specs=[pl.BlockSpec(memory_space=pltpu.SMEM)],
        out_specs=pl.BlockSpec((1, 1) + shape[2:], lambda l, h: (l, h, 0, 0, 0)),
        out_shape=jax.ShapeDtypeStruct(shape, F32),
        compiler_params=_cparams("parallel", "parallel"),
        name="bias_table",
    )(rpb.reshape(-1))


class _Stream:
    def __init__(self, n_rows, seq, mod_row):
        assert n_rows % seq == 0
        self.n_rows, self.seq, self.mod_row = n_rows, seq, mod_row

    def block(self, preferred):
        bm = min(preferred, self.seq)
        assert self.seq % bm == 0
        return bm

    def mod_spec(self, d, chunk, bm):
        return pl.BlockSpec((1, 1, d), lambda i, *_: (self.mod_row(i * bm), 0, chunk))


def _modulated_norm(x, g_ref, sh_ref, sc_ref):
    return (_rms(x, g_ref[...]) * (1 + sc_ref[0]) + sh_ref[0]).astype(BF16)


def _swap_half_pairs(x):
    lane = lax.broadcasted_iota(jnp.int32, (1, HEAD_DIM), 1)
    first = (lane % (HEAD_DIM // 2)) < HEAD_DIM // 4
    return jnp.where(first, pltpu.roll(x, HEAD_DIM - HEAD_DIM // 4, 1), pltpu.roll(x, HEAD_DIM // 4, 1))


def _inproj_kernel(*refs, rope, prenorm):
    refs = list(refs)
    if prenorm:
        x_ref, g_ref, sh_ref, sc_ref = refs[:4]
        del refs[:4]
        h_ref = refs.pop()
        h_ref[...] = _modulated_norm(x_ref[...], g_ref, sh_ref, sc_ref)
    else:
        h_ref = refs.pop(0)
    w_ref, qn_ref, kn_ref = refs[:3]
    del refs[:3]
    if rope:
        cos_ref, sin_ref = refs[:2]
        del refs[:2]
    out_refs = refs

    chunks, col = [], 0
    for kind, width, o_ref in zip(QKV_KIND, QKV_SIZES, out_refs):
        chunks += [(kind, o_ref, c, col + c) for c in range(0, width, 2 * HEAD_DIM)]
        col += width

    chunks.sort(key=lambda chunk: -(2 * chunk[0][1] + (chunk[0][2] and rope) + chunk[0][0]))

    def project(chunk):
        return _dot(h_ref[...], w_ref[:, chunk[3]:chunk[3] + 2 * HEAD_DIM])

    nxt = project(chunks[0])
    for idx, ((is_q, has_norm, has_rope), o_ref, c, _) in enumerate(chunks):
        y = nxt
        if idx + 1 < len(chunks):
            nxt = project(chunks[idx + 1])
        for s in range(2):
            yh = y[:, s * HEAD_DIM:(s + 1) * HEAD_DIM]
            if has_norm:
                yh = _rms(yh, (qn_ref if is_q else kn_ref)[...])
            if has_rope and rope:
                yh = yh * cos_ref[...] + _swap_half_pairs(yh) * sin_ref[...]
            if is_q:
                yh = yh * QSCALE
            o_ref[:, c + s * HEAD_DIM:c + (s + 1) * HEAD_DIM] = yh.astype(BF16)


def _inproj(st, h, w_in, layer, qn, kn, rope_tabs, prenorm=None):
    bm = st.block(512)
    rope = rope_tabs is not None
    row = lambda width: pl.BlockSpec((bm, width), lambda i: (i, 0))
    vec = pl.BlockSpec((1, HEAD_DIM), lambda i: (0, 0))
    out_widths = list(QKV_SIZES)
    if prenorm is not None:
        x, g, mod = prenorm
        d = x.shape[1]
        in_specs = [row(d), pl.BlockSpec((1, d), lambda i: (0, 0)), st.mod_spec(d, 0, bm), st.mod_spec(d, 1, bm)]
        args = [x, g.reshape(1, d), mod, mod]
        out_widths.append(d)
    else:
        d = h.shape[1]
        in_specs, args = [row(d)], [h]
    in_specs += [pl.BlockSpec((None, d, QKV_COLS), lambda i: (layer, 0, 0), pipeline_mode=pl.Buffered(1)), vec, vec]
    args += [w_in, qn.reshape(1, HEAD_DIM), kn.reshape(1, HEAD_DIM)]
    if rope:
        per_seq = st.seq // bm
        tab = pl.BlockSpec((bm, HEAD_DIM), lambda i: (i % per_seq, 0))
        in_specs += [tab, tab]
        args += list(rope_tabs)
    return pl.pallas_call(
        functools.partial(_inproj_kernel, rope=rope, prenorm=prenorm is not None),
        grid=(st.n_rows // bm,),
        in_specs=in_specs,
        out_specs=[row(width) for width in out_widths],
        out_shape=[jax.ShapeDtypeStruct((st.n_rows, width), BF16) for width in out_widths],
        compiler_params=_cparams("parallel"),
        name="inproj",
    )(*args)


def _softmax_pv(s, sx, v, vx, sink=None):
    m = jnp.maximum(jnp.max(s, axis=-1, keepdims=True), jnp.max(sx, axis=-1, keepdims=True))
    if sink is not None:
        m = jnp.maximum(m, sink)
    e = jnp.exp2(s - m)
    ex = jnp.exp2(sx - m)
    den = jnp.sum(e, axis=-1, keepdims=True) + jnp.sum(ex, axis=-1, keepdims=True)
    if sink is not None:
        den = den + jnp.exp2(sink - m)
    o = _dot(e.astype(BF16), v) + _dot(ex.astype(BF16), vx)
    return o / den


def _attn_a_kernel(sink_ref, q_ref, k_ref, v_ref, kx_ref, vx_ref, o_ref, *, bq, seq):
    i = pl.program_id(1)
    wlen = bq + 2 * WINDOW
    q0 = i * bq
    start = pl.multiple_of(jnp.clip(q0 - WINDOW, 0, seq - wlen), WINDOW)
    qpos = q0 + lax.broadcasted_iota(jnp.int32, (bq, 1), 0)
    kpos = start + lax.broadcasted_iota(jnp.int32, (1, wlen), 1)
    valid = jnp.abs(kpos - qpos) <= WINDOW
    group = A_HEADS // A_KV

    def kv_cols(h):
        kv = h // group
        return slice(kv * HEAD_DIM, (kv + 1) * HEAD_DIM)

    def scores(h):
        q = q_ref[:, h * HEAD_DIM:(h + 1) * HEAD_DIM]
        return _dot_t(q, k_ref[pl.ds(start, wlen), kv_cols(h)]), _dot_t(q, kx_ref[:, kv_cols(h)])

    nxt = scores(0)
    for h in range(A_HEADS):
        s, sx = nxt
        if h + 1 < A_HEADS:
            nxt = scores(h + 1)
        o = _softmax_pv(jnp.where(valid, s, NEG), sx, v_ref[pl.ds(start, wlen), kv_cols(h)], vx_ref[:, kv_cols(h)],
                        sink_ref[h] * LOG2E)
        o_ref[:, h * HEAD_DIM:(h + 1) * HEAD_DIM] = o.astype(BF16)


def _attn_a(q, k, v, kx, vx, sink, batch, seq, ctx_len):
    bq = 256
    assert seq % bq == 0 and seq >= bq + 2 * WINDOW
    nq = seq // bq
    qspec = pl.BlockSpec((bq, A_WIDTH), lambda b, i: (b * nq + i, 0))
    kvspec = pl.BlockSpec((seq, A_KV * HEAD_DIM), lambda b, i: (b, 0))
    xspec = pl.BlockSpec((ctx_len, A_KV * HEAD_DIM), lambda b, i: (b, 0))
    return pl.pallas_call(
        functools.partial(_attn_a_kernel, bq=bq, seq=seq),
        grid=(batch, nq),
        in_specs=[pl.BlockSpec(memory_space=pltpu.SMEM), qspec, kvspec, kvspec, xspec, xspec],
        out_specs=qspec,
        out_shape=jax.ShapeDtypeStruct(q.shape, BF16),
        compiler_params=_cparams("parallel", "parallel"),
        name="attn_window",
    )(sink, q, k, v, kx, vx)


def _attn_b_kernel(q_ref, k_ref, v_ref, kx_ref, vx_ref, tab_ref, o_ref, *, rows, rsub):
    rb = pl.program_id(2)
    n_rb = pl.num_programs(2)
    typ = jnp.where(rb == 0, 0, jnp.where(rb == n_rb - 1, 2, 1))
    ks = jnp.clip(rb * NB_ROWS - NA_H // 2, 0, rows - NB_KROWS)
    start = pl.multiple_of(ks * GRID_W, GRID_W)
    n_keys = NB_KROWS * GRID_W
    kw, vw = k_ref[pl.ds(start, n_keys), :], v_ref[pl.ds(start, n_keys), :]
    kx, vx = kx_ref[...], vx_ref[...]
    bq = q_ref.shape[0]

    def scores(r):
        q = q_ref[r:r + rsub, :]
        return _dot_t(q, kw), _dot_t(q, kx)

    nxt = scores(0)
    for r in range(0, bq, rsub):
        s, sx = nxt
        if r + rsub < bq:
            nxt = scores(r + rsub)
        o = _softmax_pv(s + tab_ref[typ, r:r + rsub, :], sx, vw, vx)
        o_ref[r:r + rsub, :] = o.astype(BF16)


def _attn_b(q, k, v, kx, vx, table, layer, batch, seq, ctx_len):
    rows = seq // GRID_W
    assert seq % GRID_W == 0 and rows % NB_ROWS == 0 and rows >= NB_KROWS
    nrb = rows // NB_ROWS
    bq = NB_ROWS * GRID_W
    qspec = pl.BlockSpec((bq, HEAD_DIM), lambda h, b, rb: (b * nrb + rb, h))
    kvspec = pl.BlockSpec((seq, HEAD_DIM), lambda h, b, rb: (b, h))
    xspec = pl.BlockSpec((ctx_len, HEAD_DIM), lambda h, b, rb: (b, h))
    tspec = pl.BlockSpec((None, None, 3, bq, NB_KROWS * GRID_W), lambda h, b, rb: (layer, h, 0, 0, 0))
    return pl.pallas_call(
        functools.partial(_attn_b_kernel, rows=rows, rsub=128),
        grid=(B_HEADS, batch, nrb),
        in_specs=[qspec, kvspec, kvspec, xspec, xspec, tspec],
        out_specs=qspec,
        out_shape=jax.ShapeDtypeStruct(q.shape, BF16),
        compiler_params=_cparams("parallel", "parallel", "parallel"),
        name="attn_neighbourhood",
    )(q, k, v, kx, vx, table)


def _attn_c_kernel(q_ref, k_ref, v_ref, kx_ref, vx_ref, o_ref, *, chunk):
    n_lat = k_ref.shape[0] // chunk

    def keys(c):
        return kx_ref[...] if c == 0 else k_ref[(c - 1) * chunk:c * chunk, :]

    def values(c):
        return vx_ref[...] if c == 0 else v_ref[(c - 1) * chunk:c * chunk, :]

    def scores(unit):
        g, c = unit
        return _dot_t(q_ref[:, g * HEAD_DIM:(g + 1) * HEAD_DIM], keys(c))

    units = [(g, c) for g in range(C_HEADS // C_KV) for c in range(n_lat + 1)]
    s_next = scores(units[0])
    m = l = acc = None
    for idx, (g, c) in enumerate(units):
        s = s_next
        if idx + 1 < len(units):
            s_next = scores(units[idx + 1])
        m_blk = jnp.max(s, axis=-1, keepdims=True)
        if c == 0:
            m = m_blk
            p = jnp.exp2(s - m)
            l = jnp.sum(p, axis=-1, keepdims=True)
            acc = _dot(p.astype(BF16), values(c))
        else:
            m_new = jnp.maximum(m, m_blk)
            alpha = jnp.exp2(m - m_new)
            p = jnp.exp2(s - m_new)
            l = alpha * l + jnp.sum(p, axis=-1, keepdims=True)
            acc = alpha * acc + _dot(p.astype(BF16), values(c))
            m = m_new
        if c == n_lat:
            o_ref[:, g * HEAD_DIM:(g + 1) * HEAD_DIM] = (acc / l).astype(BF16)


def _attn_c(q, k, v, kx, vx, batch, seq, ctx_len):
    bq = 512
    assert seq % bq == 0
    nq = seq // bq
    gw = (C_HEADS // C_KV) * HEAD_DIM
    qspec = pl.BlockSpec((bq, gw), lambda b, kv, i: (b * nq + i, kv))
    kvspec = pl.BlockSpec((seq, HEAD_DIM), lambda b, kv, i: (b, kv))
    xspec = pl.BlockSpec((ctx_len, HEAD_DIM), lambda b, kv, i: (b, kv))
    return pl.pallas_call(
        functools.partial(_attn_c_kernel, chunk=1024),
        grid=(batch, C_KV, nq),
        in_specs=[qspec, kvspec, kvspec, xspec, xspec],
        out_specs=qspec,
        out_shape=jax.ShapeDtypeStruct(q.shape, BF16),
        compiler_params=_cparams("parallel", "parallel", "parallel"),
        name="attn_global",
    )(q, k, v, kx, vx)


def _softmax_pv_dense(s, v, sink=None):
    m = jnp.max(s, axis=-1, keepdims=True)
    if sink is not None:
        m = jnp.maximum(m, sink)
    e = jnp.exp2(s - m)
    den = jnp.sum(e, axis=-1, keepdims=True)
    if sink is not None:
        den = den + jnp.exp2(sink - m)
    return _dot(e.astype(BF16), v) / den


def _attn_ctx_kernel(sink_ref, qa_ref, ka_ref, va_ref, qb_ref, kb_ref, vb_ref, qc_ref, kc_ref, vc_ref,
                     oa_ref, ob_ref, oc_ref):
    def head(ref, h):
        return ref[:, h * HEAD_DIM:(h + 1) * HEAD_DIM]

    def put(ref, h, o):
        ref[:, h * HEAD_DIM:(h + 1) * HEAD_DIM] = o.astype(BF16)

    ga, gc = A_HEADS // A_KV, C_HEADS // C_KV
    for h in range(A_HEADS):
        s = _dot_t(head(qa_ref, h), head(ka_ref, h // ga))
        put(oa_ref, h, _softmax_pv_dense(s, head(va_ref, h // ga), sink_ref[h] * LOG2E))
    for h in range(B_HEADS):
        put(ob_ref, h, _softmax_pv_dense(_dot_t(head(qb_ref, h), head(kb_ref, h)), head(vb_ref, h)))
    for h in range(C_HEADS):
        s = _dot_t(head(qc_ref, h), head(kc_ref, h // gc))
        put(oc_ref, h, _softmax_pv_dense(s, head(vc_ref, h // gc)))


def _attn_ctx(qkv, sink, batch, ctx_len):
    spec = lambda width: pl.BlockSpec((ctx_len, width), lambda b: (b, 0))
    out_w = (A_WIDTH, B_WIDTH, C_WIDTH)
    return pl.pallas_call(
        _attn_ctx_kernel,
        grid=(batch,),
        in_specs=[pl.BlockSpec(memory_space=pltpu.SMEM)] + [spec(w) for w in QKV_SIZES],
        out_specs=[spec(w) for w in out_w],
        out_shape=[jax.ShapeDtypeStruct((batch * ctx_len, w), BF16) for w in out_w],
        compiler_params=_cparams("parallel"),
        name="attn_ctx",
    )(sink, *qkv)


def _gate_merge_kernel(h_ref, oa_ref, ob_ref, oc_ref, wga_ref, wgb_ref, wgc_ref, wb_ref, m_ref, *, rsub, csub):
    bm, bj = m_ref.shape
    branches = ((oa_ref, wga_ref, 0, A_WIDTH), (ob_ref, wgb_ref, A_WIDTH, A_WIDTH + B_WIDTH),
                (oc_ref, wgc_ref, A_WIDTH + B_WIDTH, A_WIDTH + B_WIDTH + C_WIDTH))

    def dots(unit):
        r, c = unit
        h = h_ref[r:r + rsub, :]
        return [(_dot(h, wg_ref[:, c:c + csub]), _dot(o_ref[r:r + rsub, :], wb_ref[lo:hi, c:c + csub]))
                for o_ref, wg_ref, lo, hi in branches]

    units = [(r, c) for c in range(0, bj, csub) for r in range(0, bm, rsub)]
    nxt = dots(units[0])
    for idx, (r, c) in enumerate(units):
        cur = nxt
        if idx + 1 < len(units):
            nxt = dots(units[idx + 1])
        (ga, ba), (gb, bb), (gc, bc) = cur
        m = jax.nn.sigmoid(ga) * ba + jax.nn.sigmoid(gb) * bb + jax.nn.sigmoid(gc) * bc
        m_ref[r:r + rsub, c:c + csub] = m.astype(BF16)


def _gate_merge(st, h, o_a, o_b, o_c, w_in, w_branch, layer):
    d = h.shape[1]
    bm, bj = st.block(1024), 512
    nj = d // bj
    gate0 = QKV_COLS // bj
    row = lambda width: pl.BlockSpec((bm, width), lambda i, j: (i, 0))
    gspec = lambda k: pl.BlockSpec((None, d, bj), lambda i, j: (layer, 0, gate0 + k * nj + j))
    return pl.pallas_call(
        functools.partial(_gate_merge_kernel, rsub=min(bm, 512), csub=256),
        grid=(st.n_rows // bm, nj),
        in_specs=[row(d), row(A_WIDTH), row(B_WIDTH), row(C_WIDTH), gspec(0), gspec(1), gspec(2),
                  pl.BlockSpec((None, w_branch.shape[1], bj), lambda i, j: (layer, 0, j))],
        out_specs=pl.BlockSpec((bm, bj), lambda i, j: (i, j)),
        out_shape=jax.ShapeDtypeStruct(h.shape, BF16),
        compiler_params=_cparams("parallel", "parallel"),
        name="gate_merge",
    )(h, o_a, o_b, o_c, w_in, w_in, w_in, w_branch)


def _resproj_kernel(*refs, last):
    x_ref, a_ref, w_ref, gate_ref, gn_ref = refs[:5]
    xn = x_ref[...] + gate_ref[0] * _dot(a_ref[...], w_ref[...])
    if last:
        refs[5][...] = _rms(xn, gn_ref[...])
    else:
        shn_ref, scn_ref, xo_ref, hn_ref = refs[5:]
        xo_ref[...] = xn
        hn_ref[...] = _modulated_norm(xn, gn_ref, shn_ref, scn_ref)


def _resproj(st, x, a, w, layer, mod, gate_chunk, g_next, mod_next, next_chunks):
    d = x.shape[1]
    k = a.shape[1]
    bm = st.block(256)
    last = mod_next is None
    row = lambda width: pl.BlockSpec((bm, width), lambda i: (i, 0))
    in_specs = [row(d), row(k),
                pl.BlockSpec((None, k, d), lambda i: (layer, 0, 0), pipeline_mode=pl.Buffered(1)),
                st.mod_spec(d, gate_chunk, bm), pl.BlockSpec((1, d), lambda i: (0, 0))]
    args = [x, a, w, mod, g_next.reshape(1, d)]
    if last:
        out_specs = row(d)
        out_shape = jax.ShapeDtypeStruct(x.shape, F32)
    else:
        in_specs += [st.mod_spec(d, next_chunks[0], bm), st.mod_spec(d, next_chunks[1], bm)]
        args += [mod_next, mod_next]
        out_specs = [row(d), row(d)]
        out_shape = [jax.ShapeDtypeStruct(x.shape, F32), jax.ShapeDtypeStruct(x.shape, BF16)]
    return pl.pallas_call(
        functools.partial(_resproj_kernel, last=last),
        grid=(st.n_rows // bm,),
        in_specs=in_specs,
        out_specs=out_specs,
        out_shape=out_shape,
        compiler_params=_cparams("parallel"),
        name="resproj",
    )(*args)


HALO = 16


def _convglu_up_kernel(h_ref, hp_ref, hn_ref, wa_ref, wu_ref, cw_ref, cb_ref, g_ref, hext_ref, *, seq, rsub, csub):
    i, j = pl.program_id(0), pl.program_id(1)
    bm, bn = g_ref.shape

    @pl.when(j == 0)
    def _():
        pos = (i * bm) % seq
        zeros = jnp.zeros_like(hp_ref)
        hext_ref[:HALO, :] = jnp.where(pos == 0, zeros, hp_ref[...])
        hext_ref[HALO:HALO + bm, :] = h_ref[...]
        hext_ref[HALO + bm:, :] = jnp.where(pos + bm == seq, zeros, hn_ref[...])

    n_ext = rsub + 2 * HALO

    def dots(unit):
        r, c = unit
        return (_dot(hext_ref[r:r + n_ext, :], wa_ref[:, c:c + csub]), _dot(h_ref[r:r + rsub, :], wu_ref[:, c:c + csub]))

    units = [(r, c) for c in range(0, bn, csub) for r in range(0, bm, rsub)]
    nxt = dots(units[0])
    for idx, (r, c) in enumerate(units):
        a_ext, u = nxt
        if idx + 1 < len(units):
            nxt = dots(units[idx + 1])
        a = (pltpu.roll(a_ext, 1, 0) * cw_ref[0:1, c:c + csub] + a_ext * cw_ref[1:2, c:c + csub]
             + pltpu.roll(a_ext, n_ext - 1, 0) * cw_ref[2:3, c:c + csub])[HALO:HALO + rsub] + cb_ref[:, c:c + csub]
        g_ref[r:r + rsub, c:c + csub] = (a * jax.nn.sigmoid(a) * u).astype(BF16)


def _convglu_up(st, h2, w_up, cw, cb, layer):
    d = h2.shape[1]
    ff = w_up.shape[2] // 2
    bm, bn = st.block(1024), 512
    rsub, csub = min(bm, 512), 256
    assert ff % bn == 0 and bm % HALO == 0
    nj = ff // bn
    per_blk = bm // HALO
    n_halo = st.n_rows // HALO
    return pl.pallas_call(
        functools.partial(_convglu_up_kernel, seq=st.seq, rsub=rsub, csub=csub),
        grid=(st.n_rows // bm, nj),
        in_specs=[pl.BlockSpec((bm, d), lambda i, j: (i, 0)),
                  pl.BlockSpec((HALO, d), lambda i, j: (jnp.maximum(i * per_blk - 1, 0), 0)),
                  pl.BlockSpec((HALO, d), lambda i, j: (jnp.minimum((i + 1) * per_blk, n_halo - 1), 0)),
                  pl.BlockSpec((None, d, bn), lambda i, j: (layer, 0, j)),
                  pl.BlockSpec((None, d, bn), lambda i, j: (layer, 0, nj + j)),
                  pl.BlockSpec((None, CONV_W, bn), lambda i, j: (layer, 0, j)),
                  pl.BlockSpec((None, 1, bn), lambda i, j: (layer, 0, j))],
        out_specs=pl.BlockSpec((bm, bn), lambda i, j: (i, j)),
        out_shape=jax.ShapeDtypeStruct((st.n_rows, ff), BF16),
        scratch_shapes=[pltpu.VMEM((bm + 2 * HALO, d), BF16)],
        compiler_params=_cparams("parallel", "arbitrary"),
        name="convglu_up",
    )(h2, h2, h2, w_up, w_up, cw, cb.reshape(cb.shape[0], 1, ff))


def _rope_tables(seq):
    t = jnp.arange(seq)
    row = (t // GRID_W).astype(F32)
    col = (t % GRID_W).astype(F32)
    quarter = HEAD_DIM // 4
    inv = ROPE_THETA ** (-jnp.arange(quarter, dtype=F32) / quarter)
    ang_r = row[:, None] * inv[None, :]
    ang_c = col[:, None] * inv[None, :]
    cr, sr, cc, sc = jnp.cos(ang_r), jnp.sin(ang_r), jnp.cos(ang_c), jnp.sin(ang_c)
    return (jnp.concatenate([cr, cr, cc, cc], axis=-1), jnp.concatenate([-sr, sr, -sc, sc], axis=-1))


def kernel(x, c, ctx, c_ctx, w_ada, b_ada, norm1, w_in, sink_a, rpb_b, qnorm_c, knorm_c, w_branch, w_out,
           norm2, w_up, conv_w, conv_b, w_down, final_norm):
    batch, seq, d = x.shape
    ctx_len = ctx.shape[1]
    depth = w_ada.shape[0]
    assert batch + 1 <= MOD_ROWS

    lat = _Stream(batch * seq, seq, lambda r: r // seq)
    cst = _Stream(batch * ctx_len, ctx_len, lambda r: batch)

    s = jnp.concatenate([c, c_ctx[None, :], jnp.zeros((MOD_ROWS - batch - 1, d), F32)], axis=0)
    mods = _ada(s, w_ada, b_ada).reshape(depth, MOD_ROWS, 1, 6 * d)
    table = _bias_table(rpb_b)
    rope_tabs = _rope_tables(seq)
    w_in, w_branch, w_out, w_up, w_down = (w.astype(BF16) for w in (w_in, w_branch, w_out, w_up, w_down))

    def mixer_tail(st, xs, hs, o_a, o_b, o_c, l):
        m = _gate_merge(st, hs, o_a, o_b, o_c, w_in, w_branch, l)
        xs, h2 = _resproj(st, xs, m, w_out, l, mods[l], 2, norm2[l], mods[l], (3, 4))
        g = _convglu_up(st, h2, w_up, conv_w, conv_b, l)
        if l == depth - 1:
            return _resproj(st, xs, g, w_down, l, mods[l], 5, final_norm, None, None)
        return _resproj(st, xs, g, w_down, l, mods[l], 5, norm1[l + 1], mods[l + 1], (0, 1))

    xl = x.reshape(batch * seq, d)
    xc = ctx.reshape(batch * ctx_len, d)
    h = hc = out = None
    for l in range(depth):
        if l == 0:
            *qkv, h = _inproj(lat, None, w_in, l, qnorm_c[l], knorm_c[l], rope_tabs, (xl, norm1[l], mods[l]))
            *qkv_x, hc = _inproj(cst, None, w_in, l, qnorm_c[l], knorm_c[l], None, (xc, norm1[l], mods[l]))
        else:
            qkv = _inproj(lat, h, w_in, l, qnorm_c[l], knorm_c[l], rope_tabs)
            qkv_x = _inproj(cst, hc, w_in, l, qnorm_c[l], knorm_c[l], None)
        qa, ka, va, qb, kb, vb, qc, kc, vc = qkv
        _, ka_x, va_x, _, kb_x, vb_x, _, kc_x, vc_x = qkv_x
        o_a = _attn_a(qa, ka, va, ka_x, va_x, sink_a[l], batch, seq, ctx_len)
        o_b = _attn_b(qb, kb, vb, kb_x, vb_x, table, l, batch, seq, ctx_len)
        o_c = _attn_c(qc, kc, vc, kc_x, vc_x, batch, seq, ctx_len)
        if l == depth - 1:
            out = mixer_tail(lat, xl, h, o_a, o_b, o_c, l)
        else:
            xl, h = mixer_tail(lat, xl, h, o_a, o_b, o_c, l)
            oa_x, ob_x, oc_x = _attn_ctx(qkv_x, sink_a[l], batch, ctx_len)
            xc, hc = mixer_tail(cst, xc, hc, oa_x, ob_x, oc_x, l)
    return out.reshape(batch, seq, d)
```

```python
import functools

import jax
import jax.numpy as jnp
from jax import lax
from jax.experimental import pallas as pl
from jax.experimental.pallas import tpu as pltpu

F32 = jnp.float32
BF16 = jnp.bfloat16

GRID_W = 64
HEAD_DIM = 128
A_HEADS, A_KV = 6, 2
B_HEADS = 4
C_HEADS, C_KV = 6, 2
WINDOW = 128
NA_H, NA_W = 8, 16
ROPE_THETA = 10000.0
CONV_W = 3
N_BRANCH = 3
EPS = 1e-6
NEG = -1e30
LOG2E = 1.4426950408889634
QSCALE = HEAD_DIM ** -0.5 * LOG2E
A_WIDTH = A_HEADS * HEAD_DIM
B_WIDTH = B_HEADS * HEAD_DIM
C_WIDTH = C_HEADS * HEAD_DIM
QKV_SIZES = (A_HEADS * HEAD_DIM, A_KV * HEAD_DIM, A_KV * HEAD_DIM,
             B_HEADS * HEAD_DIM, B_HEADS * HEAD_DIM, B_HEADS * HEAD_DIM,
             C_HEADS * HEAD_DIM, C_KV * HEAD_DIM, C_KV * HEAD_DIM)
QKV_COLS = sum(QKV_SIZES)
QKV_KIND = ((True, False, True), (False, False, True), (False, False, False),
            (True, False, False), (False, False, False), (False, False, False),
            (True, True, True), (False, True, True), (False, False, False))
MOD_ROWS = 8
NB_ROWS = 8
NB_KROWS = 16
V7X_VMEM_BYTES = 64 * 1024 * 1024
VMEM_LIMIT = V7X_VMEM_BYTES - 8 * 1024 * 1024


def _cparams(*sem):
    return pltpu.CompilerParams(dimension_semantics=sem, vmem_limit_bytes=VMEM_LIMIT)


def _dot(a, b):
    return jnp.dot(a, b, preferred_element_type=F32)


def _dot_t(a, b):
    return lax.dot_general(a, b, (((1,), (1,)), ((), ())), preferred_element_type=F32)


def _rms(x, g):
    return x * lax.rsqrt(jnp.mean(x * x, axis=-1, keepdims=True) + EPS) * g


def _ada_kernel(s_ref, w_ref, b_ref, o_ref):
    s = s_ref[...]
    s = s * jax.nn.sigmoid(s)
    o_ref[0] = _dot(s.astype(BF16), w_ref[0].astype(BF16)) + b_ref[0]


def _ada(s, w_ada, b_ada):
    depth, d, n = w_ada.shape
    bn = 1024
    return pl.pallas_call(
        _ada_kernel,
        grid=(depth, n // bn),
        in_specs=[pl.BlockSpec((MOD_ROWS, d), lambda l, j: (0, 0)),
                  pl.BlockSpec((1, d, bn), lambda l, j: (l, 0, j)),
                  pl.BlockSpec((1, 1, bn), lambda l, j: (l, 0, j))],
        out_specs=pl.BlockSpec((1, MOD_ROWS, bn), lambda l, j: (l, 0, j)),
        out_shape=jax.ShapeDtypeStruct((depth, MOD_ROWS, n), F32),
        compiler_params=_cparams("parallel", "parallel"),
        name="ada",
    )(s, w_ada, b_ada.reshape(depth, 1, n))


def _bias_table_kernel(rpb_ref, o_ref):
    l, h = pl.program_id(0), pl.program_id(1)
    n_dr, n_dc = 2 * NA_H - 1, 2 * NA_W - 1
    base = (l * B_HEADS + h) * n_dr * n_dc
    qc = lax.broadcasted_iota(jnp.int32, (GRID_W, GRID_W), 0)
    kc = lax.broadcasted_iota(jnp.int32, (GRID_W, GRID_W), 1)
    cs = jnp.clip(qc - NA_W // 2, 0, GRID_W - NA_W)
    col_valid = (kc >= cs) & (kc < cs + NA_W)
    dc = jnp.clip(kc - qc + NA_W - 1, 0, 2 * NA_W - 2)
    neg = jnp.full((GRID_W, GRID_W), NEG, F32)
    toe = []
    for d in range(n_dr):
        t = jnp.zeros((GRID_W, GRID_W), F32)
        for e in range(n_dc):
            t = jnp.where(dc == e, rpb_ref[base + d * n_dc + e] * LOG2E, t)
        toe.append(jnp.where(col_valid, t, neg))
    half = NA_H // 2
    for typ in range(3):
        for a in range(NB_ROWS):
            if typ == 0:
                rs_rel, dr0 = max(a - half, 0), NA_H - 1 - a
            elif typ == 1:
                rs_rel, dr0 = a, NA_H - 1 - half - a
            else:
                rs_rel, dr0 = half + min(a, half), NA_H - 1 - NB_ROWS - a
            for jj in range(NB_KROWS):
                valid = rs_rel <= jj < rs_rel + NA_H
                o_ref[0, 0, typ, a * GRID_W:(a + 1) * GRID_W, jj * GRID_W:(jj + 1) * GRID_W] = (
                    toe[jj + dr0] if valid else neg)


def _bias_table(rpb):
    depth = rpb.shape[0]
    shape = (depth, B_HEADS, 3, NB_ROWS * GRID_W, NB_KROWS * GRID_W)
    return pl.pallas_call(
        _bias_table_kernel,
        grid=(depth, B_HEADS),
        in_specs=[pl.BlockSpec(memory_space=pltpu.SMEM)],
        out_specs=pl.BlockSpec((1, 1) + shape[2:], lambda l, h: (l, h, 0, 0, 0)),
        out_shape=jax.ShapeDtypeStruct(shape, F32),
        compiler_params=_cparams("parallel", "parallel"),
        name="bias_table",
    )(rpb.reshape(-1))


class _Stream:
    def __init__(self, n_rows, seq, mod_row):
        assert n_rows % seq == 0
        self.n_rows, self.seq, self.mod_row = n_rows, seq, mod_row

    def block(self, preferred):
        bm = min(preferred, self.seq)
        assert self.seq % bm == 0
        return bm

    def mod_spec(self, d, chunk, bm):
        return pl.BlockSpec((1, 1, d), lambda i, *_: (self.mod_row(i * bm), 0, chunk))


def _modulated_norm(x, g_ref, sh_ref, sc_ref):
    return (_rms(x, g_ref[...]) * (1 + sc_ref[0]) + sh_ref[0]).astype(BF16)


def _swap_half_pairs(x):
    lane = lax.broadcasted_iota(jnp.int32, (1, HEAD_DIM), 1)
    first = (lane % (HEAD_DIM // 2)) < HEAD_DIM // 4
    return jnp.where(first, pltpu.roll(x, HEAD_DIM - HEAD_DIM // 4, 1), pltpu.roll(x, HEAD_DIM // 4, 1))


def _inproj_kernel(*refs, rope, prenorm):
    refs = list(refs)
    if prenorm:
        x_ref, g_ref, sh_ref, sc_ref = refs[:4]
        del refs[:4]
        h_ref = refs.pop()
        h_ref[...] = _modulated_norm(x_ref[...], g_ref, sh_ref, sc_ref)
    else:
        h_ref = refs.pop(0)
    w_ref, qn_ref, kn_ref = refs[:3]
    del refs[:3]
    if rope:
        cos_ref, sin_ref = refs[:2]
        del refs[:2]
    out_refs = refs

    chunks, col = [], 0
    for kind, width, o_ref in zip(QKV_KIND, QKV_SIZES, out_refs):
        chunks += [(kind, o_ref, c, col + c) for c in range(0, width, 2 * HEAD_DIM)]
        col += width

    chunks.sort(key=lambda chunk: -(2 * chunk[0][1] + (chunk[0][2] and rope) + chunk[0][0]))

    def project(chunk):
        return _dot(h_ref[...], w_ref[:, chunk[3]:chunk[3] + 2 * HEAD_DIM])

    nxt = project(chunks[0])
    for idx, ((is_q, has_norm, has_rope), o_ref, c, _) in enumerate(chunks):
        y = nxt
        if idx + 1 < len(chunks):
            nxt = project(chunks[idx + 1])
        for s in range(2):
            yh = y[:, s * HEAD_DIM:(s + 1) * HEAD_DIM]
            if has_norm:
                yh = _rms(yh, (qn_ref if is_q else kn_ref)[...])
            if has_rope and rope:
                yh = yh * cos_ref[...] + _swap_half_pairs(yh) * sin_ref[...]
            if is_q:
                yh = yh * QSCALE
            o_ref[:, c + s * HEAD_DIM:c + (s + 1) * HEAD_DIM] = yh.astype(BF16)


def _inproj(st, h, w_in, layer, qn, kn, rope_tabs, prenorm=None):
    bm = st.block(512)
    rope = rope_tabs is not None
    row = lambda width: pl.BlockSpec((bm, width), lambda i: (i, 0))
    vec = pl.BlockSpec((1, HEAD_DIM), lambda i: (0, 0))
    out_widths = list(QKV_SIZES)
    if prenorm is not None:
        x, g, mod = prenorm
        d = x.shape[1]
        in_specs = [row(d), pl.BlockSpec((1, d), lambda i: (0, 0)), st.mod_spec(d, 0, bm), st.mod_spec(d, 1, bm)]
        args = [x, g.reshape(1, d), mod, mod]
        out_widths.append(d)
    else:
        d = h.shape[1]
        in_specs, args = [row(d)], [h]
    in_specs += [pl.BlockSpec((None, d, QKV_COLS), lambda i: (layer, 0, 0), pipeline_mode=pl.Buffered(1)), vec, vec]
    args += [w_in, qn.reshape(1, HEAD_DIM), kn.reshape(1, HEAD_DIM)]
    if rope:
        per_seq = st.seq // bm
        tab = pl.BlockSpec((bm, HEAD_DIM), lambda i: (i % per_seq, 0))
        in_specs += [tab, tab]
        args += list(rope_tabs)
    return pl.pallas_call(
        functools.partial(_inproj_kernel, rope=rope, prenorm=prenorm is not None),
        grid=(st.n_rows // bm,),
        in_specs=in_specs,
        out_specs=[row(width) for width in out_widths],
        out_shape=[jax.ShapeDtypeStruct((st.n_rows, width), BF16) for width in out_widths],
        compiler_params=_cparams("parallel"),
        name="inproj",
    )(*args)


def _softmax_pv(s, sx, v, vx, sink=None):
    m = jnp.maximum(jnp.max(s, axis=-1, keepdims=True), jnp.max(sx, axis=-1, keepdims=True))
    if sink is not None:
        m = jnp.maximum(m, sink)
    e = jnp.exp2(s - m)
    ex = jnp.exp2(sx - m)
    den = jnp.sum(e, axis=-1, keepdims=True) + jnp.sum(ex, axis=-1, keepdims=True)
    if sink is not None:
        den = den + jnp.exp2(sink - m)
    o = _dot(e.astype(BF16), v) + _dot(ex.astype(BF16), vx)
    return o / den


def _attn_a_kernel(sink_ref, q_ref, k_ref, v_ref, kx_ref, vx_ref, o_ref, *, bq, seq):
    i = pl.program_id(1)
    wlen = bq + 2 * WINDOW
    q0 = i * bq
    start = pl.multiple_of(jnp.clip(q0 - WINDOW, 0, seq - wlen), WINDOW)
    qpos = q0 + lax.broadcasted_iota(jnp.int32, (bq, 1), 0)
    kpos = start + lax.broadcasted_iota(jnp.int32, (1, wlen), 1)
    valid = jnp.abs(kpos - qpos) <= WINDOW
    group = A_HEADS // A_KV

    def kv_cols(h):
        kv = h // group
        return slice(kv * HEAD_DIM, (kv + 1) * HEAD_DIM)

    def scores(h):
        q = q_ref[:, h * HEAD_DIM:(h + 1) * HEAD_DIM]
        return _dot_t(q, k_ref[pl.ds(start, wlen), kv_cols(h)]), _dot_t(q, kx_ref[:, kv_cols(h)])

    nxt = scores(0)
    for h in range(A_HEADS):
        s, sx = nxt
        if h + 1 < A_HEADS:
            nxt = scores(h + 1)
        o = _softmax_pv(jnp.where(valid, s, NEG), sx, v_ref[pl.ds(start, wlen), kv_cols(h)], vx_ref[:, kv_cols(h)],
                        sink_ref[h] * LOG2E)
        o_ref[:, h * HEAD_DIM:(h + 1) * HEAD_DIM] = o.astype(BF16)


def _attn_a(q, k, v, kx, vx, sink, batch, seq, ctx_len):
    bq = 256
    assert seq % bq == 0 and seq >= bq + 2 * WINDOW
    nq = seq // bq
    qspec = pl.BlockSpec((bq, A_WIDTH), lambda b, i: (b * nq + i, 0))
    kvspec = pl.BlockSpec((seq, A_KV * HEAD_DIM), lambda b, i: (b, 0))
    xspec = pl.BlockSpec((ctx_len, A_KV * HEAD_DIM), lambda b, i: (b, 0))
    return pl.pallas_call(
        functools.partial(_attn_a_kernel, bq=bq, seq=seq),
        grid=(batch, nq),
        in_specs=[pl.BlockSpec(memory_space=pltpu.SMEM), qspec, kvspec, kvspec, xspec, xspec],
        out_specs=qspec,
        out_shape=jax.ShapeDtypeStruct(q.shape, BF16),
        compiler_params=_cparams("parallel", "parallel"),
        name="attn_window",
    )(sink, q, k, v, kx, vx)


def _attn_b_kernel(q_ref, k_ref, v_ref, kx_ref, vx_ref, tab_ref, o_ref, *, rows, rsub):
    rb = pl.program_id(2)
    n_rb = pl.num_programs(2)
    typ = jnp.where(rb == 0, 0, jnp.where(rb == n_rb - 1, 2, 1))
    ks = jnp.clip(rb * NB_ROWS - NA_H // 2, 0, rows - NB_KROWS)
    start = pl.multiple_of(ks * GRID_W, GRID_W)
    n_keys = NB_KROWS * GRID_W
    bq = q_ref.shape[0]
    heads = q_ref.shape[1] // HEAD_DIM

    def cols(h):
        return slice(h * HEAD_DIM, (h + 1) * HEAD_DIM)

    def scores(unit):
        h, r = unit
        q = q_ref[r:r + rsub, cols(h)]
        return _dot_t(q, k_ref[pl.ds(start, n_keys), cols(h)]), _dot_t(q, kx_ref[:, cols(h)])

    units = [(h, r) for h in range(heads) for r in range(0, bq, rsub)]
    nxt = scores(units[0])
    for idx, (h, r) in enumerate(units):
        s, sx = nxt
        if idx + 1 < len(units):
            nxt = scores(units[idx + 1])
        o = _softmax_pv(s + tab_ref[h, typ, r:r + rsub, :], sx, v_ref[pl.ds(start, n_keys), cols(h)], vx_ref[:, cols(h)])
        o_ref[r:r + rsub, cols(h)] = o.astype(BF16)


def _attn_b(q, k, v, kx, vx, table, layer, batch, seq, ctx_len):
    rows = seq // GRID_W
    assert seq % GRID_W == 0 and rows % NB_ROWS == 0 and rows >= NB_KROWS
    nrb = rows // NB_ROWS
    bq = NB_ROWS * GRID_W
    hps = B_HEADS
    assert B_HEADS % hps == 0
    qspec = pl.BlockSpec((bq, hps * HEAD_DIM), lambda h, b, rb: (b * nrb + rb, h))
    kvspec = pl.BlockSpec((seq, hps * HEAD_DIM), lambda h, b, rb: (b, h))
    xspec = pl.BlockSpec((ctx_len, hps * HEAD_DIM), lambda h, b, rb: (b, h))
    tspec = pl.BlockSpec((None, hps, 3, bq, NB_KROWS * GRID_W), lambda h, b, rb: (layer, h, 0, 0, 0),
                         pipeline_mode=pl.Buffered(1))
    return pl.pallas_call(
        functools.partial(_attn_b_kernel, rows=rows, rsub=256),
        grid=(B_HEADS // hps, batch, nrb),
        in_specs=[qspec, kvspec, kvspec, xspec, xspec, tspec],
        out_specs=qspec,
        out_shape=jax.ShapeDtypeStruct(q.shape, BF16),
        compiler_params=_cparams("parallel", "parallel", "parallel"),
        name="attn_neighbourhood",
    )(q, k, v, kx, vx, table)


def _attn_c_kernel(q_ref, k_ref, v_ref, kx_ref, vx_ref, o_ref, vt_ref, *, chunk):
    ctx_len = kx_ref.shape[0]
    n_lat = k_ref.shape[0] // chunk

    @pl.when(pl.program_id(2) == 0)
    def _():
        vt_ref[:, :ctx_len] = vx_ref[...].T
        vt_ref[:, ctx_len:] = v_ref[...].T

    def keys(c):
        return kx_ref[...] if c == 0 else k_ref[(c - 1) * chunk:c * chunk, :]

    def values_t(c):
        return vt_ref[:, :ctx_len] if c == 0 else vt_ref[:, ctx_len + (c - 1) * chunk:ctx_len + c * chunk]

    def scores(unit):
        g, c = unit
        return _dot_t(keys(c), q_ref[:, g * HEAD_DIM:(g + 1) * HEAD_DIM])

    units = [(g, c) for g in range(C_HEADS // C_KV) for c in range(n_lat + 1)]
    s_next = scores(units[0])
    m = l = acc = None
    for idx, (g, c) in enumerate(units):
        s = s_next
        if idx + 1 < len(units):
            s_next = scores(units[idx + 1])
        m_blk = jnp.max(s, axis=0, keepdims=True)
        if c == 0:
            m = m_blk
            p = jnp.exp2(s - m)
            l = jnp.sum(p, axis=0, keepdims=True)
            acc = _dot(values_t(c), p.astype(BF16))
        else:
            m_new = jnp.maximum(m, m_blk)
            alpha = jnp.exp2(m - m_new)
            p = jnp.exp2(s - m_new)
            l = alpha * l + jnp.sum(p, axis=0, keepdims=True)
            acc = alpha * acc + _dot(values_t(c), p.astype(BF16))
            m = m_new
        if c == n_lat:
            o_ref[:, g * HEAD_DIM:(g + 1) * HEAD_DIM] = (acc / l).T.astype(BF16)


def _attn_c(q, k, v, kx, vx, batch, seq, ctx_len):
    bq = 512
    assert seq % bq == 0
    nq = seq // bq
    gw = (C_HEADS // C_KV) * HEAD_DIM
    qspec = pl.BlockSpec((bq, gw), lambda b, kv, i: (b * nq + i, kv))
    kvspec = pl.BlockSpec((seq, HEAD_DIM), lambda b, kv, i: (b, kv))
    xspec = pl.BlockSpec((ctx_len, HEAD_DIM), lambda b, kv, i: (b, kv))
    return pl.pallas_call(
        functools.partial(_attn_c_kernel, chunk=1024),
        grid=(batch, C_KV, nq),
        in_specs=[qspec, kvspec, kvspec, xspec, xspec],
        out_specs=qspec,
        out_shape=jax.ShapeDtypeStruct(q.shape, BF16),
        scratch_shapes=[pltpu.VMEM((HEAD_DIM, ctx_len + seq), BF16)],
        compiler_params=_cparams("parallel", "parallel", "arbitrary"),
        name="attn_global",
    )(q, k, v, kx, vx)


def _softmax_pv_dense(s, v, sink=None):
    m = jnp.max(s, axis=-1, keepdims=True)
    if sink is not None:
        m = jnp.maximum(m, sink)
    e = jnp.exp2(s - m)
    den = jnp.sum(e, axis=-1, keepdims=True)
    if sink is not None:
        den = den + jnp.exp2(sink - m)
    return _dot(e.astype(BF16), v) / den


def _attn_ctx_kernel(sink_ref, qa_ref, ka_ref, va_ref, qb_ref, kb_ref, vb_ref, qc_ref, kc_ref, vc_ref,
                     oa_ref, ob_ref, oc_ref):
    def head(ref, h):
        return ref[:, h * HEAD_DIM:(h + 1) * HEAD_DIM]

    def put(ref, h, o):
        ref[:, h * HEAD_DIM:(h + 1) * HEAD_DIM] = o.astype(BF16)

    ga, gc = A_HEADS // A_KV, C_HEADS // C_KV
    for h in range(A_HEADS):
        s = _dot_t(head(qa_ref, h), head(ka_ref, h // ga))
        put(oa_ref, h, _softmax_pv_dense(s, head(va_ref, h // ga), sink_ref[h] * LOG2E))
    for h in range(B_HEADS):
        put(ob_ref, h, _softmax_pv_dense(_dot_t(head(qb_ref, h), head(kb_ref, h)), head(vb_ref, h)))
    for h in range(C_HEADS):
        s = _dot_t(head(qc_ref, h), head(kc_ref, h // gc))
        put(oc_ref, h, _softmax_pv_dense(s, head(vc_ref, h // gc)))


def _attn_ctx(qkv, sink, batch, ctx_len):
    spec = lambda width: pl.BlockSpec((ctx_len, width), lambda b: (b, 0))
    out_w = (A_WIDTH, B_WIDTH, C_WIDTH)
    return pl.pallas_call(
        _attn_ctx_kernel,
        grid=(batch,),
        in_specs=[pl.BlockSpec(memory_space=pltpu.SMEM)] + [spec(w) for w in QKV_SIZES],
        out_specs=[spec(w) for w in out_w],
        out_shape=[jax.ShapeDtypeStruct((batch * ctx_len, w), BF16) for w in out_w],
        compiler_params=_cparams("parallel"),
        name="attn_ctx",
    )(sink, *qkv)


def _gate_merge_kernel(h_ref, oa_ref, ob_ref, oc_ref, wga_ref, wgb_ref, wgc_ref, wb_ref, m_ref, *, rsub, csub):
    bm, bj = m_ref.shape
    branches = ((oa_ref, wga_ref, 0, A_WIDTH), (ob_ref, wgb_ref, A_WIDTH, A_WIDTH + B_WIDTH),
                (oc_ref, wgc_ref, A_WIDTH + B_WIDTH, A_WIDTH + B_WIDTH + C_WIDTH))

    def dots(unit):
        r, c = unit
        h = h_ref[r:r + rsub, :]
        return [(_dot(h, wg_ref[:, c:c + csub]), _dot(o_ref[r:r + rsub, :], wb_ref[lo:hi, c:c + csub]))
                for o_ref, wg_ref, lo, hi in branches]

    units = [(r, c) for c in range(0, bj, csub) for r in range(0, bm, rsub)]
    nxt = dots(units[0])
    for idx, (r, c) in enumerate(units):
        cur = nxt
        if idx + 1 < len(units):
            nxt = dots(units[idx + 1])
        (ga, ba), (gb, bb), (gc, bc) = cur
        m = jax.nn.sigmoid(ga) * ba + jax.nn.sigmoid(gb) * bb + jax.nn.sigmoid(gc) * bc
        m_ref[r:r + rsub, c:c + csub] = m.astype(BF16)


def _gate_merge(st, h, o_a, o_b, o_c, w_in, w_branch, layer):
    d = h.shape[1]
    bm, bj = st.block(1024), 512
    nj = d // bj
    gate0 = QKV_COLS // bj
    row = lambda width: pl.BlockSpec((bm, width), lambda i, j: (i, 0))
    gspec = lambda k: pl.BlockSpec((None, d, bj), lambda i, j: (layer, 0, gate0 + k * nj + j))
    return pl.pallas_call(
        functools.partial(_gate_merge_kernel, rsub=min(bm, 512), csub=256),
        grid=(st.n_rows // bm, nj),
        in_specs=[row(d), row(A_WIDTH), row(B_WIDTH), row(C_WIDTH), gspec(0), gspec(1), gspec(2),
                  pl.BlockSpec((None, w_branch.shape[1], bj), lambda i, j: (layer, 0, j))],
        out_specs=pl.BlockSpec((bm, bj), lambda i, j: (i, j)),
        out_shape=jax.ShapeDtypeStruct(h.shape, BF16),
        compiler_params=_cparams("parallel", "parallel"),
        name="gate_merge",
    )(h, o_a, o_b, o_c, w_in, w_in, w_in, w_branch)


def _resproj_kernel(*refs, last):
    x_ref, a_ref, w_ref, gate_ref, gn_ref = refs[:5]
    xn = x_ref[...] + gate_ref[0] * _dot(a_ref[...], w_ref[...])
    if last:
        refs[5][...] = _rms(xn, gn_ref[...])
    else:
        shn_ref, scn_ref, xo_ref, hn_ref = refs[5:]
        xo_ref[...] = xn
        hn_ref[...] = _modulated_norm(xn, gn_ref, shn_ref, scn_ref)


def _resproj(st, x, a, w, layer, mod, gate_chunk, g_next, mod_next, next_chunks):
    d = x.shape[1]
    k = a.shape[1]
    bm = st.block(256)
    last = mod_next is None
    row = lambda width: pl.BlockSpec((bm, width), lambda i: (i, 0))
    in_specs = [row(d), row(k),
                pl.BlockSpec((None, k, d), lambda i: (layer, 0, 0), pipeline_mode=pl.Buffered(1)),
                st.mod_spec(d, gate_chunk, bm), pl.BlockSpec((1, d), lambda i: (0, 0))]
    args = [x, a, w, mod, g_next.reshape(1, d)]
    if last:
        out_specs = row(d)
        out_shape = jax.ShapeDtypeStruct(x.shape, F32)
    else:
        in_specs += [st.mod_spec(d, next_chunks[0], bm), st.mod_spec(d, next_chunks[1], bm)]
        args += [mod_next, mod_next]
        out_specs = [row(d), row(d)]
        out_shape = [jax.ShapeDtypeStruct(x.shape, F32), jax.ShapeDtypeStruct(x.shape, BF16)]
    return pl.pallas_call(
        functools.partial(_resproj_kernel, last=last),
        grid=(st.n_rows // bm,),
        in_specs=in_specs,
        out_specs=out_specs,
        out_shape=out_shape,
        compiler_params=_cparams("parallel"),
        name="resproj",
    )(*args)


HALO = 16


def _convglu_up_kernel(h_ref, hp_ref, hn_ref, wa_ref, wu_ref, cw_ref, cb_ref, g_ref, hext_ref, *, seq, rsub, csub):
    i, j = pl.program_id(0), pl.program_id(1)
    bm, bn = g_ref.shape

    @pl.when(j == 0)
    def _():
        pos = (i * bm) % seq
        zeros = jnp.zeros_like(hp_ref)
        hext_ref[:HALO, :] = jnp.where(pos == 0, zeros, hp_ref[...])
        hext_ref[HALO:HALO + bm, :] = h_ref[...]
        hext_ref[HALO + bm:, :] = jnp.where(pos + bm == seq, zeros, hn_ref[...])

    n_ext = rsub + 2 * HALO

    def gate_dot(unit):
        r, c = unit
        return _dot(hext_ref[r:r + n_ext, :], wa_ref[:, c:c + csub])

    units = [(r, c) for c in range(0, bn, csub) for r in range(0, bm, rsub)]
    nxt = gate_dot(units[0])
    for idx, (r, c) in enumerate(units):
        a_ext = nxt
        if idx + 1 < len(units):
            nxt = gate_dot(units[idx + 1])
        a = (pltpu.roll(a_ext, 1, 0) * cw_ref[0:1, c:c + csub] + a_ext * cw_ref[1:2, c:c + csub]
             + pltpu.roll(a_ext, n_ext - 1, 0) * cw_ref[2:3, c:c + csub])[HALO:HALO + rsub] + cb_ref[:, c:c + csub]
        act = a * jax.nn.sigmoid(a)
        u = _dot(h_ref[r:r + rsub, :], wu_ref[:, c:c + csub])
        g_ref[r:r + rsub, c:c + csub] = (act * u).astype(BF16)


def _convglu_up(st, h2, w_up, cw, cb, layer):
    d = h2.shape[1]
    ff = w_up.shape[2] // 2
    bm, bn = st.block(1024), 512
    rsub, csub = min(bm, 512), 256
    assert ff % bn == 0 and bm % HALO == 0
    nj = ff // bn
    per_blk = bm // HALO
    n_halo = st.n_rows // HALO
    return pl.pallas_call(
        functools.partial(_convglu_up_kernel, seq=st.seq, rsub=rsub, csub=csub),
        grid=(st.n_rows // bm, nj),
        in_specs=[pl.BlockSpec((bm, d), lambda i, j: (i, 0)),
                  pl.BlockSpec((HALO, d), lambda i, j: (jnp.maximum(i * per_blk - 1, 0), 0)),
                  pl.BlockSpec((HALO, d), lambda i, j: (jnp.minimum((i + 1) * per_blk, n_halo - 1), 0)),
                  pl.BlockSpec((None, d, bn), lambda i, j: (layer, 0, j)),
                  pl.BlockSpec((None, d, bn), lambda i, j: (layer, 0, nj + j)),
                  pl.BlockSpec((None, CONV_W, bn), lambda i, j: (layer, 0, j)),
                  pl.BlockSpec((None, 1, bn), lambda i, j: (layer, 0, j))],
        out_specs=pl.BlockSpec((bm, bn), lambda i, j: (i, j)),
        out_shape=jax.ShapeDtypeStruct((st.n_rows, ff), BF16),
        scratch_shapes=[pltpu.VMEM((bm + 2 * HALO, d), BF16)],
        compiler_params=_cparams("parallel", "arbitrary"),
        name="convglu_up",
    )(h2, h2, h2, w_up, w_up, cw, cb.reshape(cb.shape[0], 1, ff))


def _rope_tables(seq):
    t = jnp.arange(seq)
    row = (t // GRID_W).astype(F32)
    col = (t % GRID_W).astype(F32)
    quarter = HEAD_DIM // 4
    inv = ROPE_THETA ** (-jnp.arange(quarter, dtype=F32) / quarter)
    ang_r = row[:, None] * inv[None, :]
    ang_c = col[:, None] * inv[None, :]
    cr, sr, cc, sc = jnp.cos(ang_r), jnp.sin(ang_r), jnp.cos(ang_c), jnp.sin(ang_c)
    return (jnp.concatenate([cr, cr, cc, cc], axis=-1), jnp.concatenate([-sr, sr, -sc, sc], axis=-1))


def kernel(x, c, ctx, c_ctx, w_ada, b_ada, norm1, w_in, sink_a, rpb_b, qnorm_c, knorm_c, w_branch, w_out,
           norm2, w_up, conv_w, conv_b, w_down, final_norm):
    batch, seq, d = x.shape
    ctx_len = ctx.shape[1]
    depth = w_ada.shape[0]
    assert batch + 1 <= MOD_ROWS

    lat = _Stream(batch * seq, seq, lambda r: r // seq)
    cst = _Stream(batch * ctx_len, ctx_len, lambda r: batch)

    s = jnp.concatenate([c, c_ctx[None, :], jnp.zeros((MOD_ROWS - batch - 1, d), F32)], axis=0)
    mods = _ada(s, w_ada, b_ada).reshape(depth, MOD_ROWS, 1, 6 * d)
    table = _bias_table(rpb_b)
    rope_tabs = _rope_tables(seq)
    w_in, w_branch, w_out, w_up, w_down = (w.astype(BF16) for w in (w_in, w_branch, w_out, w_up, w_down))

    def mixer_tail(st, xs, hs, o_a, o_b, o_c, l):
        m = _gate_merge(st, hs, o_a, o_b, o_c, w_in, w_branch, l)
        xs, h2 = _resproj(st, xs, m, w_out, l, mods[l], 2, norm2[l], mods[l], (3, 4))
        g = _convglu_up(st, h2, w_up, conv_w, conv_b, l)
        if l == depth - 1:
            return _resproj(st, xs, g, w_down, l, mods[l], 5, final_norm, None, None)
        return _resproj(st, xs, g, w_down, l, mods[l], 5, norm1[l + 1], mods[l + 1], (0, 1))

    xl = x.reshape(batch * seq, d)
    xc = ctx.reshape(batch * ctx_len, d)
    h = hc = out = None
    for l in range(depth):
        if l == 0:
            *qkv, h = _inproj(lat, None, w_in, l, qnorm_c[l], knorm_c[l], rope_tabs, (xl, norm1[l], mods[l]))
            *qkv_x, hc = _inproj(cst, None, w_in, l, qnorm_c[l], knorm_c[l], None, (xc, norm1[l], mods[l]))
        else:
            qkv = _inproj(lat, h, w_in, l, qnorm_c[l], knorm_c[l], rope_tabs)
            qkv_x = _inproj(cst, hc, w_in, l, qnorm_c[l], knorm_c[l], None)
        qa, ka, va, qb, kb, vb, qc, kc, vc = qkv
        _, ka_x, va_x, _, kb_x, vb_x, _, kc_x, vc_x = qkv_x
        o_a = _attn_a(qa, ka, va, ka_x, va_x, sink_a[l], batch, seq, ctx_len)
        o_b = _attn_b(qb, kb, vb, kb_x, vb_x, table, l, batch, seq, ctx_len)
        o_c = _attn_c(qc, kc, vc, kc_x, vc_x, batch, seq, ctx_len)
        if l == depth - 1:
            out = mixer_tail(lat, xl, h, o_a, o_b, o_c, l)
        else:
            xl, h = mixer_tail(lat, xl, h, o_a, o_b, o_c, l)
            oa_x, ob_x, oc_x = _attn_ctx(qkv_x, sink_a[l], batch, ctx_len)
            xc, hc = mixer_tail(cst, xc, hc, oa_x, ob_x, oc_x, l)
    return out.reshape(batch, seq, d)
```

```python
import functools

import jax
import jax.numpy as jnp
from jax import lax
from jax.experimental import pallas as pl
from jax.experimental.pallas import tpu as pltpu

F32 = jnp.float32
BF16 = jnp.bfloat16

GRID_W = 64
HEAD_DIM = 128
A_HEADS, A_KV = 6, 2
B_HEADS = 4
C_HEADS, C_KV = 6, 2
WINDOW = 128
NA_H, NA_W = 8, 16
ROPE_THETA = 10000.0
CONV_W = 3
N_BRANCH = 3
EPS = 1e-6
NEG = -1e30
LOG2E = 1.4426950408889634
QSCALE = HEAD_DIM ** -0.5 * LOG2E
A_WIDTH = A_HEADS * HEAD_DIM
B_WIDTH = B_HEADS * HEAD_DIM
C_WIDTH = C_HEADS * HEAD_DIM
QKV_SIZES = (A_HEADS * HEAD_DIM, A_KV * HEAD_DIM, A_KV * HEAD_DIM,
             B_HEADS * HEAD_DIM, B_HEADS * HEAD_DIM, B_HEADS * HEAD_DIM,
             C_HEADS * HEAD_DIM, C_KV * HEAD_DIM, C_KV * HEAD_DIM)
QKV_COLS = sum(QKV_SIZES)
QKV_KIND = ((True, False, True), (False, False, True), (False, False, False),
            (True, False, False), (False, False, False), (False, False, False),
            (True, True, True), (False, True, True), (False, False, False))
MOD_ROWS = 8
NB_ROWS = 8
NB_KROWS = 16
V7X_VMEM_BYTES = 64 * 1024 * 1024
VMEM_LIMIT = V7X_VMEM_BYTES - 8 * 1024 * 1024


def _cparams(*sem):
    return pltpu.CompilerParams(dimension_semantics=sem, vmem_limit_bytes=VMEM_LIMIT)


def _dot(a, b):
    return jnp.dot(a, b, preferred_element_type=F32)


def _dot_t(a, b):
    return lax.dot_general(a, b, (((1,), (1,)), ((), ())), preferred_element_type=F32)


def _rms(x, g):
    return x * lax.rsqrt(jnp.mean(x * x, axis=-1, keepdims=True) + EPS) * g


CAST_BLOCK_BYTES = 8 * 1024 * 1024


def _cast_kernel(w_ref, o_ref):
    o_ref[...] = w_ref[...].astype(o_ref.dtype)


def _to_bf16(w):
    depth, k, n = w.shape
    rows = depth * k
    bm = rows
    while bm * n * 4 > CAST_BLOCK_BYTES and bm % 32 == 0:
        bm //= 2
    spec = pl.BlockSpec((bm, n), lambda i: (i, 0))
    out = pl.pallas_call(
        _cast_kernel,
        grid=(rows // bm,),
        in_specs=[spec],
        out_specs=spec,
        out_shape=jax.ShapeDtypeStruct((rows, n), BF16),
        compiler_params=_cparams("parallel"),
        name="cast_bf16",
    )(w.reshape(rows, n))
    return out.reshape(depth, k, n)


def _ada_kernel(s_ref, w_ref, b_ref, o_ref):
    s = s_ref[...]
    s = s * jax.nn.sigmoid(s)
    o_ref[0] = _dot(s.astype(BF16), w_ref[0].astype(BF16)) + b_ref[0]


def _ada(s, w_ada, b_ada):
    depth, d, n = w_ada.shape
    bn = 1024
    return pl.pallas_call(
        _ada_kernel,
        grid=(depth, n // bn),
        in_specs=[pl.BlockSpec((MOD_ROWS, d), lambda l, j: (0, 0)),
                  pl.BlockSpec((1, d, bn), lambda l, j: (l, 0, j)),
                  pl.BlockSpec((1, 1, bn), lambda l, j: (l, 0, j))],
        out_specs=pl.BlockSpec((1, MOD_ROWS, bn), lambda l, j: (l, 0, j)),
        out_shape=jax.ShapeDtypeStruct((depth, MOD_ROWS, n), F32),
        compiler_params=_cparams("parallel", "parallel"),
        name="ada",
    )(s, w_ada, b_ada.reshape(depth, 1, n))


def _bias_table_kernel(rpb_ref, o_ref):
    l, h = pl.program_id(0), pl.program_id(1)
    n_dr, n_dc = 2 * NA_H - 1, 2 * NA_W - 1
    base = (l * B_HEADS + h) * n_dr * n_dc
    qc = lax.broadcasted_iota(jnp.int32, (GRID_W, GRID_W), 0)
    kc = lax.broadcasted_iota(jnp.int32, (GRID_W, GRID_W), 1)
    cs = jnp.clip(qc - NA_W // 2, 0, GRID_W - NA_W)
    col_valid = (kc >= cs) & (kc < cs + NA_W)
    dc = jnp.clip(kc - qc + NA_W - 1, 0, 2 * NA_W - 2)
    neg = jnp.full((GRID_W, GRID_W), NEG, F32)
    toe = []
    for d in range(n_dr):
        t = jnp.zeros((GRID_W, GRID_W), F32)
        for e in range(n_dc):
            t = jnp.where(dc == e, rpb_ref[base + d * n_dc + e] * LOG2E, t)
        toe.append(jnp.where(col_valid, t, neg))
    half = NA_H // 2
    for typ in range(3):
        for a in range(NB_ROWS):
            if typ == 0:
                rs_rel, dr0 = max(a - half, 0), NA_H - 1 - a
            elif typ == 1:
                rs_rel, dr0 = a, NA_H - 1 - half - a
            else:
                rs_rel, dr0 = half + min(a, half), NA_H - 1 - NB_ROWS - a
            for jj in range(NB_KROWS):
                valid = rs_rel <= jj < rs_rel + NA_H
                o_ref[0, 0, typ, a * GRID_W:(a + 1) * GRID_W, jj * GRID_W:(jj + 1) * GRID_W] = (
                    toe[jj + dr0] if valid else neg)


def _bias_table(rpb):
    depth = rpb.shape[0]
    shape = (depth, B_HEADS, 3, NB_ROWS * GRID_W, NB_KROWS * GRID_W)
    return pl.pallas_call(
        _bias_table_kernel,
        grid=(depth, B_HEADS),
        in_specs=[pl.BlockSpec(memory_space=pltpu.SMEM)],
        out_specs=pl.BlockSpec((1, 1) + shape[2:], lambda l, h: (l, h, 0, 0, 0)),
        out_shape=jax.ShapeDtypeStruct(shape, F32),
        compiler_params=_cparams("parallel", "parallel"),
        name="bias_table",
    )(rpb.reshape(-1))


class _Stream:
    def __init__(self, n_rows, seq, mod_row, shared_mod):
        assert n_rows % seq == 0
        self.n_rows, self.seq, self.mod_row, self.shared_mod = n_rows, seq, mod_row, shared_mod

    def block(self, preferred):
        bm = min(preferred, self.n_rows if self.shared_mod else self.seq)
        assert self.seq % bm == 0 or (bm % self.seq == 0 and self.n_rows % bm == 0)
        return bm

    def mod_spec(self, d, chunk, bm):
        return pl.BlockSpec((1, 1, d), lambda i, *_: (self.mod_row(i * bm), 0, chunk))


def _modulated_norm(x, g_ref, sh_ref, sc_ref):
    return (_rms(x, g_ref[...]) * (1 + sc_ref[0]) + sh_ref[0]).astype(BF16)


def _swap_half_pairs(x):
    lane = lax.broadcasted_iota(jnp.int32, (1, HEAD_DIM), 1)
    first = (lane % (HEAD_DIM // 2)) < HEAD_DIM // 4
    return jnp.where(first, pltpu.roll(x, HEAD_DIM - HEAD_DIM // 4, 1), pltpu.roll(x, HEAD_DIM // 4, 1))


def _inproj_kernel(*refs, rope, prenorm):
    refs = list(refs)
    if prenorm:
        x_ref, g_ref, sh_ref, sc_ref = refs[:4]
        del refs[:4]
        h_ref = refs.pop()
        h_ref[...] = _modulated_norm(x_ref[...], g_ref, sh_ref, sc_ref)
    else:
        h_ref = refs.pop(0)
    w_ref, qn_ref, kn_ref = refs[:3]
    del refs[:3]
    if rope:
        cos_ref, sin_ref = refs[:2]
        del refs[:2]
    out_refs = refs

    chunks, col = [], 0
    for kind, width, o_ref in zip(QKV_KIND, QKV_SIZES, out_refs):
        chunks += [(kind, o_ref, c, col + c) for c in range(0, width, 2 * HEAD_DIM)]
        col += width

    chunks.sort(key=lambda chunk: -(2 * chunk[0][1] + (chunk[0][2] and rope) + chunk[0][0]))

    def project(chunk):
        return _dot(h_ref[...], w_ref[:, chunk[3]:chunk[3] + 2 * HEAD_DIM])

    nxt = project(chunks[0])
    for idx, ((is_q, has_norm, has_rope), o_ref, c, _) in enumerate(chunks):
        y = nxt
        if idx + 1 < len(chunks):
            nxt = project(chunks[idx + 1])
        for s in range(2):
            yh = y[:, s * HEAD_DIM:(s + 1) * HEAD_DIM]
            if has_norm:
                yh = _rms(yh, (qn_ref if is_q else kn_ref)[...])
            if has_rope and rope:
                yh = yh * cos_ref[...] + _swap_half_pairs(yh) * sin_ref[...]
            if is_q:
                yh = yh * QSCALE
            o_ref[:, c + s * HEAD_DIM:c + (s + 1) * HEAD_DIM] = yh.astype(BF16)


def _inproj(st, h, w_in, layer, qn, kn, rope_tabs, prenorm=None):
    bm = st.block(512)
    rope = rope_tabs is not None
    row = lambda width: pl.BlockSpec((bm, width), lambda i: (i, 0))
    vec = pl.BlockSpec((1, HEAD_DIM), lambda i: (0, 0))
    out_widths = list(QKV_SIZES)
    if prenorm is not None:
        x, g, mod = prenorm
        d = x.shape[1]
        in_specs = [row(d), pl.BlockSpec((1, d), lambda i: (0, 0)), st.mod_spec(d, 0, bm), st.mod_spec(d, 1, bm)]
        args = [x, g.reshape(1, d), mod, mod]
        out_widths.append(d)
    else:
        d = h.shape[1]
        in_specs, args = [row(d)], [h]
    in_specs += [pl.BlockSpec((None, d, QKV_COLS), lambda i: (layer, 0, 0), pipeline_mode=pl.Buffered(1)), vec, vec]
    args += [w_in, qn.reshape(1, HEAD_DIM), kn.reshape(1, HEAD_DIM)]
    if rope:
        per_seq = st.seq // bm
        tab = pl.BlockSpec((bm, HEAD_DIM), lambda i: (i % per_seq, 0))
        in_specs += [tab, tab]
        args += list(rope_tabs)
    return pl.pallas_call(
        functools.partial(_inproj_kernel, rope=rope, prenorm=prenorm is not None),
        grid=(st.n_rows // bm,),
        in_specs=in_specs,
        out_specs=[row(width) for width in out_widths],
        out_shape=[jax.ShapeDtypeStruct((st.n_rows, width), BF16) for width in out_widths],
        compiler_params=_cparams("parallel"),
        name="inproj",
    )(*args)


def _softmax_pv(s, sx, v, vx, sink=None):
    m = jnp.maximum(jnp.max(s, axis=-1, keepdims=True), jnp.max(sx, axis=-1, keepdims=True))
    if sink is not None:
        m = jnp.maximum(m, sink)
    e = jnp.exp2(s - m)
    ex = jnp.exp2(sx - m)
    den = jnp.sum(e, axis=-1, keepdims=True) + jnp.sum(ex, axis=-1, keepdims=True)
    if sink is not None:
        den = den + jnp.exp2(sink - m)
    o = _dot(e.astype(BF16), v) + _dot(ex.astype(BF16), vx)
    return o / den


def _attn_a_kernel(sink_ref, q_ref, k_ref, v_ref, kx_ref, vx_ref, o_ref, *, bq, seq):
    i = pl.program_id(1)
    wlen = bq + 2 * WINDOW
    q0 = i * bq
    start = pl.multiple_of(jnp.clip(q0 - WINDOW, 0, seq - wlen), WINDOW)
    qpos = q0 + lax.broadcasted_iota(jnp.int32, (bq, 1), 0)
    kpos = start + lax.broadcasted_iota(jnp.int32, (1, wlen), 1)
    valid = jnp.abs(kpos - qpos) <= WINDOW
    group = A_HEADS // A_KV

    def kv_cols(h):
        kv = h // group
        return slice(kv * HEAD_DIM, (kv + 1) * HEAD_DIM)

    def scores(h):
        q = q_ref[:, h * HEAD_DIM:(h + 1) * HEAD_DIM]
        return _dot_t(q, k_ref[pl.ds(start, wlen), kv_cols(h)]), _dot_t(q, kx_ref[:, kv_cols(h)])

    nxt = scores(0)
    for h in range(A_HEADS):
        s, sx = nxt
        if h + 1 < A_HEADS:
            nxt = scores(h + 1)
        o = _softmax_pv(jnp.where(valid, s, NEG), sx, v_ref[pl.ds(start, wlen), kv_cols(h)], vx_ref[:, kv_cols(h)],
                        sink_ref[h] * LOG2E)
        o_ref[:, h * HEAD_DIM:(h + 1) * HEAD_DIM] = o.astype(BF16)


def _attn_a(q, k, v, kx, vx, sink, batch, seq, ctx_len):
    bq = 256
    assert seq % bq == 0 and seq >= bq + 2 * WINDOW
    nq = seq // bq
    qspec = pl.BlockSpec((bq, A_WIDTH), lambda b, i: (b * nq + i, 0))
    kvspec = pl.BlockSpec((seq, A_KV * HEAD_DIM), lambda b, i: (b, 0))
    xspec = pl.BlockSpec((ctx_len, A_KV * HEAD_DIM), lambda b, i: (b, 0))
    return pl.pallas_call(
        functools.partial(_attn_a_kernel, bq=bq, seq=seq),
        grid=(batch, nq),
        in_specs=[pl.BlockSpec(memory_space=pltpu.SMEM), qspec, kvspec, kvspec, xspec, xspec],
        out_specs=qspec,
        out_shape=jax.ShapeDtypeStruct(q.shape, BF16),
        compiler_params=_cparams("parallel", "parallel"),
        name="attn_window",
    )(sink, q, k, v, kx, vx)


def _attn_b_kernel(q_ref, k_ref, v_ref, kx_ref, vx_ref, tab_ref, o_ref, *, rows, rsub):
    rb = pl.program_id(2)
    n_rb = pl.num_programs(2)
    typ = jnp.where(rb == 0, 0, jnp.where(rb == n_rb - 1, 2, 1))
    ks = jnp.clip(rb * NB_ROWS - NA_H // 2, 0, rows - NB_KROWS)
    start = pl.multiple_of(ks * GRID_W, GRID_W)
    n_keys = NB_KROWS * GRID_W
    bq = q_ref.shape[0]
    heads = q_ref.shape[1] // HEAD_DIM

    def cols(h):
        return slice(h * HEAD_DIM, (h + 1) * HEAD_DIM)

    def scores(unit):
        h, r = unit
        q = q_ref[r:r + rsub, cols(h)]
        return _dot_t(q, k_ref[pl.ds(start, n_keys), cols(h)]), _dot_t(q, kx_ref[:, cols(h)])

    units = [(h, r) for h in range(heads) for r in range(0, bq, rsub)]
    nxt = scores(units[0])
    for idx, (h, r) in enumerate(units):
        s, sx = nxt
        if idx + 1 < len(units):
            nxt = scores(units[idx + 1])
        o = _softmax_pv(s + tab_ref[h, typ, r:r + rsub, :], sx, v_ref[pl.ds(start, n_keys), cols(h)], vx_ref[:, cols(h)])
        o_ref[r:r + rsub, cols(h)] = o.astype(BF16)


def _attn_b(q, k, v, kx, vx, table, layer, batch, seq, ctx_len):
    rows = seq // GRID_W
    assert seq % GRID_W == 0 and rows % NB_ROWS == 0 and rows >= NB_KROWS
    nrb = rows // NB_ROWS
    bq = NB_ROWS * GRID_W
    hps = B_HEADS
    assert B_HEADS % hps == 0
    qspec = pl.BlockSpec((bq, hps * HEAD_DIM), lambda h, b, rb: (b * nrb + rb, h))
    kvspec = pl.BlockSpec((seq, hps * HEAD_DIM), lambda h, b, rb: (b, h))
    xspec = pl.BlockSpec((ctx_len, hps * HEAD_DIM), lambda h, b, rb: (b, h))
    tspec = pl.BlockSpec((None, hps, 3, bq, NB_KROWS * GRID_W), lambda h, b, rb: (layer, h, 0, 0, 0),
                         pipeline_mode=pl.Buffered(1))
    return pl.pallas_call(
        functools.partial(_attn_b_kernel, rows=rows, rsub=256),
        grid=(B_HEADS // hps, batch, nrb),
        in_specs=[qspec, kvspec, kvspec, xspec, xspec, tspec],
        out_specs=qspec,
        out_shape=jax.ShapeDtypeStruct(q.shape, BF16),
        compiler_params=_cparams("parallel", "parallel", "parallel"),
        name="attn_neighbourhood",
    )(q, k, v, kx, vx, table)


def _attn_c_kernel(q_ref, k_ref, v_ref, kx_ref, vx_ref, o_ref, vt_ref, *, chunk):
    ctx_len = kx_ref.shape[0]
    n_lat = k_ref.shape[0] // chunk

    @pl.when(pl.program_id(2) == 0)
    def _():
        vt_ref[:, :ctx_len] = vx_ref[...].T
        vt_ref[:, ctx_len:] = v_ref[...].T

    def keys(c):
        return kx_ref[...] if c == 0 else k_ref[(c - 1) * chunk:c * chunk, :]

    def values_t(c):
        return vt_ref[:, :ctx_len] if c == 0 else vt_ref[:, ctx_len + (c - 1) * chunk:ctx_len + c * chunk]

    def scores(unit):
        g, c = unit
        return _dot_t(keys(c), q_ref[:, g * HEAD_DIM:(g + 1) * HEAD_DIM])

    units = [(g, c) for g in range(C_HEADS // C_KV) for c in range(n_lat + 1)]
    s_next = scores(units[0])
    m = l = acc = None
    for idx, (g, c) in enumerate(units):
        s = s_next
        if idx + 1 < len(units):
            s_next = scores(units[idx + 1])
        m_blk = jnp.max(s, axis=0, keepdims=True)
        if c == 0:
            m = m_blk
            p = jnp.exp2(s - m)
            l = jnp.sum(p, axis=0, keepdims=True)
            acc = _dot(values_t(c), p.astype(BF16))
        else:
            m_new = jnp.maximum(m, m_blk)
            alpha = jnp.exp2(m - m_new)
            p = jnp.exp2(s - m_new)
            l = alpha * l + jnp.sum(p, axis=0, keepdims=True)
            acc = alpha * acc + _dot(values_t(c), p.astype(BF16))
            m = m_new
        if c == n_lat:
            o_ref[:, g * HEAD_DIM:(g + 1) * HEAD_DIM] = (acc / l).T.astype(BF16)


def _attn_c(q, k, v, kx, vx, batch, seq, ctx_len):
    bq = 512
    assert seq % bq == 0
    nq = seq // bq
    gw = (C_HEADS // C_KV) * HEAD_DIM
    qspec = pl.BlockSpec((bq, gw), lambda b, kv, i: (b * nq + i, kv))
    kvspec = pl.BlockSpec((seq, HEAD_DIM), lambda b, kv, i: (b, kv))
    xspec = pl.BlockSpec((ctx_len, HEAD_DIM), lambda b, kv, i: (b, kv))
    return pl.pallas_call(
        functools.partial(_attn_c_kernel, chunk=1024),
        grid=(batch, C_KV, nq),
        in_specs=[qspec, kvspec, kvspec, xspec, xspec],
        out_specs=qspec,
        out_shape=jax.ShapeDtypeStruct(q.shape, BF16),
        scratch_shapes=[pltpu.VMEM((HEAD_DIM, ctx_len + seq), BF16)],
        compiler_params=_cparams("parallel", "parallel", "arbitrary"),
        name="attn_global",
    )(q, k, v, kx, vx)


def _softmax_pv_dense(s, v, sink=None):
    m = jnp.max(s, axis=-1, keepdims=True)
    if sink is not None:
        m = jnp.maximum(m, sink)
    e = jnp.exp2(s - m)
    den = jnp.sum(e, axis=-1, keepdims=True)
    if sink is not None:
        den = den + jnp.exp2(sink - m)
    return _dot(e.astype(BF16), v) / den


def _attn_ctx_kernel(sink_ref, qa_ref, ka_ref, va_ref, qb_ref, kb_ref, vb_ref, qc_ref, kc_ref, vc_ref,
                     oa_ref, ob_ref, oc_ref):
    def head(ref, h):
        return ref[:, h * HEAD_DIM:(h + 1) * HEAD_DIM]

    def put(ref, h, o):
        ref[:, h * HEAD_DIM:(h + 1) * HEAD_DIM] = o.astype(BF16)

    ga, gc = A_HEADS // A_KV, C_HEADS // C_KV
    for h in range(A_HEADS):
        s = _dot_t(head(qa_ref, h), head(ka_ref, h // ga))
        put(oa_ref, h, _softmax_pv_dense(s, head(va_ref, h // ga), sink_ref[h] * LOG2E))
    for h in range(B_HEADS):
        put(ob_ref, h, _softmax_pv_dense(_dot_t(head(qb_ref, h), head(kb_ref, h)), head(vb_ref, h)))
    for h in range(C_HEADS):
        s = _dot_t(head(qc_ref, h), head(kc_ref, h // gc))
        put(oc_ref, h, _softmax_pv_dense(s, head(vc_ref, h // gc)))


def _attn_ctx(qkv, sink, batch, ctx_len):
    spec = lambda width: pl.BlockSpec((ctx_len, width), lambda b: (b, 0))
    out_w = (A_WIDTH, B_WIDTH, C_WIDTH)
    return pl.pallas_call(
        _attn_ctx_kernel,
        grid=(batch,),
        in_specs=[pl.BlockSpec(memory_space=pltpu.SMEM)] + [spec(w) for w in QKV_SIZES],
        out_specs=[spec(w) for w in out_w],
        out_shape=[jax.ShapeDtypeStruct((batch * ctx_len, w), BF16) for w in out_w],
        compiler_params=_cparams("parallel"),
        name="attn_ctx",
    )(sink, *qkv)


def _gate_merge_kernel(h_ref, oa_ref, ob_ref, oc_ref, wga_ref, wgb_ref, wgc_ref, wb_ref, m_ref, *, rsub, csub):
    bm, bj = m_ref.shape
    branches = ((oa_ref, wga_ref, 0, A_WIDTH), (ob_ref, wgb_ref, A_WIDTH, A_WIDTH + B_WIDTH),
                (oc_ref, wgc_ref, A_WIDTH + B_WIDTH, A_WIDTH + B_WIDTH + C_WIDTH))

    def dots(unit):
        r, c = unit
        h = h_ref[r:r + rsub, :]
        return [(_dot(h, wg_ref[:, c:c + csub]), _dot(o_ref[r:r + rsub, :], wb_ref[lo:hi, c:c + csub]))
                for o_ref, wg_ref, lo, hi in branches]

    units = [(r, c) for c in range(0, bj, csub) for r in range(0, bm, rsub)]
    nxt = dots(units[0])
    for idx, (r, c) in enumerate(units):
        cur = nxt
        if idx + 1 < len(units):
            nxt = dots(units[idx + 1])
        (ga, ba), (gb, bb), (gc, bc) = cur
        m = jax.nn.sigmoid(ga) * ba + jax.nn.sigmoid(gb) * bb + jax.nn.sigmoid(gc) * bc
        m_ref[r:r + rsub, c:c + csub] = m.astype(BF16)


def _gate_merge(st, h, o_a, o_b, o_c, w_in, w_branch, layer):
    d = h.shape[1]
    bm, bj = st.block(1024), 512
    nj = d // bj
    gate0 = QKV_COLS // bj
    row = lambda width: pl.BlockSpec((bm, width), lambda i, j: (i, 0))
    gspec = lambda k: pl.BlockSpec((None, d, bj), lambda i, j: (layer, 0, gate0 + k * nj + j))
    return pl.pallas_call(
        functools.partial(_gate_merge_kernel, rsub=min(bm, 512), csub=256),
        grid=(st.n_rows // bm, nj),
        in_specs=[row(d), row(A_WIDTH), row(B_WIDTH), row(C_WIDTH), gspec(0), gspec(1), gspec(2),
                  pl.BlockSpec((None, w_branch.shape[1], bj), lambda i, j: (layer, 0, j))],
        out_specs=pl.BlockSpec((bm, bj), lambda i, j: (i, j)),
        out_shape=jax.ShapeDtypeStruct(h.shape, BF16),
        compiler_params=_cparams("parallel", "parallel"),
        name="gate_merge",
    )(h, o_a, o_b, o_c, w_in, w_in, w_in, w_branch)


def _resproj_kernel(*refs, last):
    x_ref, a_ref, w_ref, gate_ref, gn_ref = refs[:5]
    xn = x_ref[...] + gate_ref[0] * _dot(a_ref[...], w_ref[...])
    if last:
        refs[5][...] = _rms(xn, gn_ref[...])
    else:
        shn_ref, scn_ref, xo_ref, hn_ref = refs[5:]
        xo_ref[...] = xn
        hn_ref[...] = _modulated_norm(xn, gn_ref, shn_ref, scn_ref)


def _resproj(st, x, a, w, layer, mod, gate_chunk, g_next, mod_next, next_chunks):
    d = x.shape[1]
    k = a.shape[1]
    bm = st.block(256)
    last = mod_next is None
    row = lambda width: pl.BlockSpec((bm, width), lambda i: (i, 0))
    in_specs = [row(d), row(k),
                pl.BlockSpec((None, k, d), lambda i: (layer, 0, 0), pipeline_mode=pl.Buffered(1)),
                st.mod_spec(d, gate_chunk, bm), pl.BlockSpec((1, d), lambda i: (0, 0))]
    args = [x, a, w, mod, g_next.reshape(1, d)]
    if last:
        out_specs = row(d)
        out_shape = jax.ShapeDtypeStruct(x.shape, F32)
    else:
        in_specs += [st.mod_spec(d, next_chunks[0], bm), st.mod_spec(d, next_chunks[1], bm)]
        args += [mod_next, mod_next]
        out_specs = [row(d), row(d)]
        out_shape = [jax.ShapeDtypeStruct(x.shape, F32), jax.ShapeDtypeStruct(x.shape, BF16)]
    return pl.pallas_call(
        functools.partial(_resproj_kernel, last=last),
        grid=(st.n_rows // bm,),
        in_specs=in_specs,
        out_specs=out_specs,
        out_shape=out_shape,
        compiler_params=_cparams("parallel"),
        name="resproj",
    )(*args)


HALO = 16


def _convglu_up_kernel(h_ref, hp_ref, hn_ref, wa_ref, wu_ref, cw_ref, cb_ref, g_ref, hext_ref, *, seq, rsub, csub):
    i, j = pl.program_id(0), pl.program_id(1)
    bm, bn = g_ref.shape

    @pl.when(j == 0)
    def _():
        pos = (i * bm) % seq
        zeros = jnp.zeros_like(hp_ref)
        hext_ref[:HALO, :] = jnp.where(pos == 0, zeros, hp_ref[...])
        hext_ref[HALO:HALO + bm, :] = h_ref[...]
        hext_ref[HALO + bm:, :] = jnp.where(pos + bm == seq, zeros, hn_ref[...])

    n_ext = rsub + 2 * HALO
    whole_seqs = bm > seq
    assert rsub == seq or not whole_seqs
    if whole_seqs:
        row = lax.broadcasted_iota(jnp.int32, (n_ext, 1), 0)
        inner = (row >= HALO) & (row < HALO + rsub)

    def gate_dot(unit):
        r, c = unit
        a_ext = _dot(hext_ref[r:r + n_ext, :], wa_ref[:, c:c + csub])
        return jnp.where(inner, a_ext, 0.0) if whole_seqs else a_ext

    units = [(r, c) for c in range(0, bn, csub) for r in range(0, bm, rsub)]
    nxt = gate_dot(units[0])
    for idx, (r, c) in enumerate(units):
        a_ext = nxt
        if idx + 1 < len(units):
            nxt = gate_dot(units[idx + 1])
        a = (pltpu.roll(a_ext, 1, 0) * cw_ref[0:1, c:c + csub] + a_ext * cw_ref[1:2, c:c + csub]
             + pltpu.roll(a_ext, n_ext - 1, 0) * cw_ref[2:3, c:c + csub])[HALO:HALO + rsub] + cb_ref[:, c:c + csub]
        act = a * jax.nn.sigmoid(a)
        u = _dot(h_ref[r:r + rsub, :], wu_ref[:, c:c + csub])
        g_ref[r:r + rsub, c:c + csub] = (act * u).astype(BF16)


def _convglu_up(st, h2, w_up, cw, cb, layer):
    d = h2.shape[1]
    ff = w_up.shape[2] // 2
    bm, bn = st.block(1024), 512
    rsub, csub = min(bm, 512, st.seq), 256
    assert ff % bn == 0 and bm % HALO == 0
    nj = ff // bn
    per_blk = bm // HALO
    n_halo = st.n_rows // HALO
    return pl.pallas_call(
        functools.partial(_convglu_up_kernel, seq=st.seq, rsub=rsub, csub=csub),
        grid=(st.n_rows // bm, nj),
        in_specs=[pl.BlockSpec((bm, d), lambda i, j: (i, 0)),
                  pl.BlockSpec((HALO, d), lambda i, j: (jnp.maximum(i * per_blk - 1, 0), 0)),
                  pl.BlockSpec((HALO, d), lambda i, j: (jnp.minimum((i + 1) * per_blk, n_halo - 1), 0)),
                  pl.BlockSpec((None, d, bn), lambda i, j: (layer, 0, j)),
                  pl.BlockSpec((None, d, bn), lambda i, j: (layer, 0, nj + j)),
                  pl.BlockSpec((None, CONV_W, bn), lambda i, j: (layer, 0, j)),
                  pl.BlockSpec((None, 1, bn), lambda i, j: (layer, 0, j))],
        out_specs=pl.BlockSpec((bm, bn), lambda i, j: (i, j)),
        out_shape=jax.ShapeDtypeStruct((st.n_rows, ff), BF16),
        scratch_shapes=[pltpu.VMEM((bm + 2 * HALO, d), BF16)],
        compiler_params=_cparams("parallel", "arbitrary"),
        name="convglu_up",
    )(h2, h2, h2, w_up, w_up, cw, cb.reshape(cb.shape[0], 1, ff))


def _rope_tables(seq):
    t = jnp.arange(seq)
    row = (t // GRID_W).astype(F32)
    col = (t % GRID_W).astype(F32)
    quarter = HEAD_DIM // 4
    inv = ROPE_THETA ** (-jnp.arange(quarter, dtype=F32) / quarter)
    ang_r = row[:, None] * inv[None, :]
    ang_c = col[:, None] * inv[None, :]
    cr, sr, cc, sc = jnp.cos(ang_r), jnp.sin(ang_r), jnp.cos(ang_c), jnp.sin(ang_c)
    return (jnp.concatenate([cr, cr, cc, cc], axis=-1), jnp.concatenate([-sr, sr, -sc, sc], axis=-1))


def kernel(x, c, ctx, c_ctx, w_ada, b_ada, norm1, w_in, sink_a, rpb_b, qnorm_c, knorm_c, w_branch, w_out,
           norm2, w_up, conv_w, conv_b, w_down, final_norm):
    batch, seq, d = x.shape
    ctx_len = ctx.shape[1]
    depth = w_ada.shape[0]
    assert batch + 1 <= MOD_ROWS

    lat = _Stream(batch * seq, seq, lambda r: r // seq, shared_mod=False)
    cst = _Stream(batch * ctx_len, ctx_len, lambda r: batch, shared_mod=True)

    s = jnp.concatenate([c, c_ctx[None, :], jnp.zeros((MOD_ROWS - batch - 1, d), F32)], axis=0)
    mods = _ada(s, w_ada, b_ada).reshape(depth, MOD_ROWS, 1, 6 * d)
    table = _bias_table(rpb_b)
    rope_tabs = _rope_tables(seq)
    w_in, w_branch, w_out, w_up, w_down = (_to_bf16(w) for w in (w_in, w_branch, w_out, w_up, w_down))

    def mixer_tail(st, xs, hs, o_a, o_b, o_c, l):
        m = _gate_merge(st, hs, o_a, o_b, o_c, w_in, w_branch, l)
        xs, h2 = _resproj(st, xs, m, w_out, l, mods[l], 2, norm2[l], mods[l], (3, 4))
        g = _convglu_up(st, h2, w_up, conv_w, conv_b, l)
        if l == depth - 1:
            return _resproj(st, xs, g, w_down, l, mods[l], 5, final_norm, None, None)
        return _resproj(st, xs, g, w_down, l, mods[l], 5, norm1[l + 1], mods[l + 1], (0, 1))

    xl = x.reshape(batch * seq, d)
    xc = ctx.reshape(batch * ctx_len, d)
    h = hc = out = None
    for l in range(depth):
        if l == 0:
            *qkv, h = _inproj(lat, None, w_in, l, qnorm_c[l], knorm_c[l], rope_tabs, (xl, norm1[l], mods[l]))
            *qkv_x, hc = _inproj(cst, None, w_in, l, qnorm_c[l], knorm_c[l], None, (xc, norm1[l], mods[l]))
        else:
            qkv = _inproj(lat, h, w_in, l, qnorm_c[l], knorm_c[l], rope_tabs)
            qkv_x = _inproj(cst, hc, w_in, l, qnorm_c[l], knorm_c[l], None)
        qa, ka, va, qb, kb, vb, qc, kc, vc = qkv
        _, ka_x, va_x, _, kb_x, vb_x, _, kc_x, vc_x = qkv_x
        o_a = _attn_a(qa, ka, va, ka_x, va_x, sink_a[l], batch, seq, ctx_len)
        o_b = _attn_b(qb, kb, vb, kb_x, vb_x, table, l, batch, seq, ctx_len)
        o_c = _attn_c(qc, kc, vc, kc_x, vc_x, batch, seq, ctx_len)
        if l == depth - 1:
            out = mixer_tail(lat, xl, h, o_a, o_b, o_c, l)
        else:
            xl, h = mixer_tail(lat, xl, h, o_a, o_b, o_c, l)
            oa_x, ob_x, oc_x = _attn_ctx(qkv_x, sink_a[l], batch, ctx_len)
            xc, hc = mixer_tail(cst, xc, hc, oa_x, ob_x, oc_x, l)
    return out.reshape(batch, seq, d)
```

```python
import functools

import jax
import jax.numpy as jnp
from jax import lax
from jax.experimental import pallas as pl
from jax.experimental.pallas import tpu as pltpu

F32 = jnp.float32
BF16 = jnp.bfloat16

GRID_W = 64
HEAD_DIM = 128
A_HEADS, A_KV = 6, 2
B_HEADS = 4
C_HEADS, C_KV = 6, 2
WINDOW = 128
NA_H, NA_W = 8, 16
ROPE_THETA = 10000.0
CONV_W = 3
N_BRANCH = 3
EPS = 1e-6
NEG = -1e30
LOG2E = 1.4426950408889634
QSCALE = HEAD_DIM ** -0.5 * LOG2E
A_WIDTH = A_HEADS * HEAD_DIM
B_WIDTH = B_HEADS * HEAD_DIM
C_WIDTH = C_HEADS * HEAD_DIM
QKV_SIZES = (A_HEADS * HEAD_DIM, A_KV * HEAD_DIM, A_KV * HEAD_DIM,
             B_HEADS * HEAD_DIM, B_HEADS * HEAD_DIM, B_HEADS * HEAD_DIM,
             C_HEADS * HEAD_DIM, C_KV * HEAD_DIM, C_KV * HEAD_DIM)
QKV_COLS = sum(QKV_SIZES)
QKV_KIND = ((True, False, True), (False, False, True), (False, False, False),
            (True, False, False), (False, False, False), (False, False, False),
            (True, True, True), (False, True, True), (False, False, False))
MOD_ROWS = 8
NB_ROWS = 8
NB_KROWS = 16
V7X_VMEM_BYTES = 64 * 1024 * 1024
VMEM_LIMIT = V7X_VMEM_BYTES - 8 * 1024 * 1024
RESIDENT_WEIGHT_BYTES_FOR_WIDE_ROWS = V7X_VMEM_BYTES // 4


def _cparams(*sem):
    return pltpu.CompilerParams(dimension_semantics=sem, vmem_limit_bytes=VMEM_LIMIT)


def _dot(a, b):
    return jnp.dot(a, b, preferred_element_type=F32)


def _dot_t(a, b):
    return lax.dot_general(a, b, (((1,), (1,)), ((), ())), preferred_element_type=F32)


def _rms(x, g):
    return x * lax.rsqrt(jnp.mean(x * x, axis=-1, keepdims=True) + EPS) * g


def _ada_kernel(s_ref, w_ref, b_ref, o_ref):
    s = s_ref[...]
    s = s * jax.nn.sigmoid(s)
    o_ref[0] = _dot(s.astype(BF16), w_ref[0].astype(BF16)) + b_ref[0]


def _ada(s, w_ada, b_ada):
    depth, d, n = w_ada.shape
    bn = 1024
    return pl.pallas_call(
        _ada_kernel,
        grid=(depth, n // bn),
        in_specs=[pl.BlockSpec((MOD_ROWS, d), lambda l, j: (0, 0)),
                  pl.BlockSpec((1, d, bn), lambda l, j: (l, 0, j)),
                  pl.BlockSpec((1, 1, bn), lambda l, j: (l, 0, j))],
        out_specs=pl.BlockSpec((1, MOD_ROWS, bn), lambda l, j: (l, 0, j)),
        out_shape=jax.ShapeDtypeStruct((depth, MOD_ROWS, n), F32),
        compiler_params=_cparams("parallel", "parallel"),
        name="ada",
    )(s, w_ada, b_ada.reshape(depth, 1, n))


def _bias_table_kernel(rpb_ref, o_ref):
    l, h = pl.program_id(0), pl.program_id(1)
    n_dr, n_dc = 2 * NA_H - 1, 2 * NA_W - 1
    base = (l * B_HEADS + h) * n_dr * n_dc
    qc = lax.broadcasted_iota(jnp.int32, (GRID_W, GRID_W), 0)
    kc = lax.broadcasted_iota(jnp.int32, (GRID_W, GRID_W), 1)
    cs = jnp.clip(qc - NA_W // 2, 0, GRID_W - NA_W)
    col_valid = (kc >= cs) & (kc < cs + NA_W)
    dc = jnp.clip(kc - qc + NA_W - 1, 0, 2 * NA_W - 2)
    neg = jnp.full((GRID_W, GRID_W), NEG, F32)
    toe = []
    for d in range(n_dr):
        t = jnp.zeros((GRID_W, GRID_W), F32)
        for e in range(n_dc):
            t = jnp.where(dc == e, rpb_ref[base + d * n_dc + e] * LOG2E, t)
        toe.append(jnp.where(col_valid, t, neg))
    half = NA_H // 2
    for typ in range(3):
        for a in range(NB_ROWS):
            if typ == 0:
                rs_rel, dr0 = max(a - half, 0), NA_H - 1 - a
            elif typ == 1:
                rs_rel, dr0 = a, NA_H - 1 - half - a
            else:
                rs_rel, dr0 = half + min(a, half), NA_H - 1 - NB_ROWS - a
            for jj in range(NB_KROWS):
                valid = rs_rel <= jj < rs_rel + NA_H
                o_ref[0, 0, typ, a * GRID_W:(a + 1) * GRID_W, jj * GRID_W:(jj + 1) * GRID_W] = (
                    toe[jj + dr0] if valid else neg)


def _bias_table(rpb):
    depth = rpb.shape[0]
    shape = (depth, B_HEADS, 3, NB_ROWS * GRID_W, NB_KROWS * GRID_W)
    return pl.pallas_call(
        _bias_table_kernel,
        grid=(depth, B_HEADS),
        in_specs=[pl.BlockSpec(memory_space=pltpu.SMEM)],
        out_specs=pl.BlockSpec((1, 1) + shape[2:], lambda l, h: (l, h, 0, 0, 0)),
        out_shape=jax.ShapeDtypeStruct(shape, F32),
        compiler_params=_cparams("parallel", "parallel"),
        name="bias_table",
    )(rpb.reshape(-1))


class _Stream:
    def __init__(self, n_rows, seq, mod_row, shared_mod):
        assert n_rows % seq == 0
        self.n_rows, self.seq, self.mod_row, self.shared_mod = n_rows, seq, mod_row, shared_mod

    def block(self, preferred):
        bm = min(preferred, self.n_rows if self.shared_mod else self.seq)
        assert self.seq % bm == 0 or (bm % self.seq == 0 and self.n_rows % bm == 0)
        return bm

    def mod_spec(self, d, chunk, bm):
        return pl.BlockSpec((1, 1, d), lambda i, *_: (self.mod_row(i * bm), 0, chunk))


def _modulated_norm(x, g_ref, sh_ref, sc_ref):
    return (_rms(x, g_ref[...]) * (1 + sc_ref[0]) + sh_ref[0]).astype(BF16)


def _swap_half_pairs(x):
    lane = lax.broadcasted_iota(jnp.int32, (1, HEAD_DIM), 1)
    first = (lane % (HEAD_DIM // 2)) < HEAD_DIM // 4
    return jnp.where(first, pltpu.roll(x, HEAD_DIM - HEAD_DIM // 4, 1), pltpu.roll(x, HEAD_DIM // 4, 1))


def _inproj_kernel(*refs, rope, prenorm):
    refs = list(refs)
    if prenorm:
        x_ref, g_ref, sh_ref, sc_ref = refs[:4]
        del refs[:4]
        h_ref = refs.pop()
        h_ref[...] = _modulated_norm(x_ref[...], g_ref, sh_ref, sc_ref)
    else:
        h_ref = refs.pop(0)
    w_ref, qn_ref, kn_ref = refs[:3]
    del refs[:3]
    if rope:
        cos_ref, sin_ref = refs[:2]
        del refs[:2]
    out_refs = refs

    chunks, col = [], 0
    for kind, width, o_ref in zip(QKV_KIND, QKV_SIZES, out_refs):
        chunks += [(kind, o_ref, c, col + c) for c in range(0, width, 2 * HEAD_DIM)]
        col += width

    chunks.sort(key=lambda chunk: -(2 * chunk[0][1] + (chunk[0][2] and rope) + chunk[0][0]))

    def project(chunk):
        return _dot(h_ref[...], w_ref[:, chunk[3]:chunk[3] + 2 * HEAD_DIM])

    nxt = project(chunks[0])
    for idx, ((is_q, has_norm, has_rope), o_ref, c, _) in enumerate(chunks):
        y = nxt
        if idx + 1 < len(chunks):
            nxt = project(chunks[idx + 1])
        for s in range(2):
            yh = y[:, s * HEAD_DIM:(s + 1) * HEAD_DIM]
            if has_norm:
                yh = _rms(yh, (qn_ref if is_q else kn_ref)[...])
            if has_rope and rope:
                yh = yh * cos_ref[...] + _swap_half_pairs(yh) * sin_ref[...]
            if is_q:
                yh = yh * QSCALE
            o_ref[:, c + s * HEAD_DIM:c + (s + 1) * HEAD_DIM] = yh.astype(BF16)


def _inproj(st, h, w_in, layer, qn, kn, rope_tabs, prenorm=None):
    bm = st.block(512)
    rope = rope_tabs is not None
    row = lambda width: pl.BlockSpec((bm, width), lambda i: (i, 0))
    vec = pl.BlockSpec((1, HEAD_DIM), lambda i: (0, 0))
    out_widths = list(QKV_SIZES)
    if prenorm is not None:
        x, g, mod = prenorm
        d = x.shape[1]
        in_specs = [row(d), pl.BlockSpec((1, d), lambda i: (0, 0)), st.mod_spec(d, 0, bm), st.mod_spec(d, 1, bm)]
        args = [x, g.reshape(1, d), mod, mod]
        out_widths.append(d)
    else:
        d = h.shape[1]
        in_specs, args = [row(d)], [h]
    in_specs += [pl.BlockSpec((None, d, QKV_COLS), lambda i: (layer, 0, 0), pipeline_mode=pl.Buffered(1)), vec, vec]
    args += [w_in, qn.reshape(1, HEAD_DIM), kn.reshape(1, HEAD_DIM)]
    if rope:
        per_seq = st.seq // bm
        tab = pl.BlockSpec((bm, HEAD_DIM), lambda i: (i % per_seq, 0))
        in_specs += [tab, tab]
        args += list(rope_tabs)
    return pl.pallas_call(
        functools.partial(_inproj_kernel, rope=rope, prenorm=prenorm is not None),
        grid=(st.n_rows // bm,),
        in_specs=in_specs,
        out_specs=[row(width) for width in out_widths],
        out_shape=[jax.ShapeDtypeStruct((st.n_rows, width), BF16) for width in out_widths],
        compiler_params=_cparams("parallel"),
        name="inproj",
    )(*args)


def _softmax_pv(s, sx, v, vx, sink=None):
    m = jnp.maximum(jnp.max(s, axis=-1, keepdims=True), jnp.max(sx, axis=-1, keepdims=True))
    if sink is not None:
        m = jnp.maximum(m, sink)
    e = jnp.exp2(s - m)
    ex = jnp.exp2(sx - m)
    den = jnp.sum(e, axis=-1, keepdims=True) + jnp.sum(ex, axis=-1, keepdims=True)
    if sink is not None:
        den = den + jnp.exp2(sink - m)
    o = _dot(e.astype(BF16), v) + _dot(ex.astype(BF16), vx)
    return o / den


def _attn_a_kernel(sink_ref, q_ref, k_ref, v_ref, kx_ref, vx_ref, o_ref, *, bq, seq):
    i = pl.program_id(1)
    wlen = bq + 2 * WINDOW
    q0 = i * bq
    start = pl.multiple_of(jnp.clip(q0 - WINDOW, 0, seq - wlen), WINDOW)
    qpos = q0 + lax.broadcasted_iota(jnp.int32, (bq, 1), 0)
    kpos = start + lax.broadcasted_iota(jnp.int32, (1, wlen), 1)
    valid = jnp.abs(kpos - qpos) <= WINDOW
    group = A_HEADS // A_KV

    def kv_cols(h):
        kv = h // group
        return slice(kv * HEAD_DIM, (kv + 1) * HEAD_DIM)

    def scores(h):
        q = q_ref[:, h * HEAD_DIM:(h + 1) * HEAD_DIM]
        return _dot_t(q, k_ref[pl.ds(start, wlen), kv_cols(h)]), _dot_t(q, kx_ref[:, kv_cols(h)])

    nxt = scores(0)
    for h in range(A_HEADS):
        s, sx = nxt
        if h + 1 < A_HEADS:
            nxt = scores(h + 1)
        o = _softmax_pv(jnp.where(valid, s, NEG), sx, v_ref[pl.ds(start, wlen), kv_cols(h)], vx_ref[:, kv_cols(h)],
                        sink_ref[h] * LOG2E)
        o_ref[:, h * HEAD_DIM:(h + 1) * HEAD_DIM] = o.astype(BF16)


def _attn_a(q, k, v, kx, vx, sink, batch, seq, ctx_len):
    bq = 256
    assert seq % bq == 0 and seq >= bq + 2 * WINDOW
    nq = seq // bq
    qspec = pl.BlockSpec((bq, A_WIDTH), lambda b, i: (b * nq + i, 0))
    kvspec = pl.BlockSpec((seq, A_KV * HEAD_DIM), lambda b, i: (b, 0))
    xspec = pl.BlockSpec((ctx_len, A_KV * HEAD_DIM), lambda b, i: (b, 0))
    return pl.pallas_call(
        functools.partial(_attn_a_kernel, bq=bq, seq=seq),
        grid=(batch, nq),
        in_specs=[pl.BlockSpec(memory_space=pltpu.SMEM), qspec, kvspec, kvspec, xspec, xspec],
        out_specs=qspec,
        out_shape=jax.ShapeDtypeStruct(q.shape, BF16),
        compiler_params=_cparams("parallel", "parallel"),
        name="attn_window",
    )(sink, q, k, v, kx, vx)


def _attn_b_kernel(q_ref, k_ref, v_ref, kx_ref, vx_ref, tab_ref, o_ref, *, rows, rsub):
    rb = pl.program_id(2)
    n_rb = pl.num_programs(2)
    typ = jnp.where(rb == 0, 0, jnp.where(rb == n_rb - 1, 2, 1))
    ks = jnp.clip(rb * NB_ROWS - NA_H // 2, 0, rows - NB_KROWS)
    start = pl.multiple_of(ks * GRID_W, GRID_W)
    n_keys = NB_KROWS * GRID_W
    bq = q_ref.shape[0]
    heads = q_ref.shape[1] // HEAD_DIM

    def cols(h):
        return slice(h * HEAD_DIM, (h + 1) * HEAD_DIM)

    def scores(unit):
        h, r = unit
        q = q_ref[r:r + rsub, cols(h)]
        return _dot_t(q, k_ref[pl.ds(start, n_keys), cols(h)]), _dot_t(q, kx_ref[:, cols(h)])

    units = [(h, r) for h in range(heads) for r in range(0, bq, rsub)]
    nxt = scores(units[0])
    for idx, (h, r) in enumerate(units):
        s, sx = nxt
        if idx + 1 < len(units):
            nxt = scores(units[idx + 1])
        o = _softmax_pv(s + tab_ref[h, typ, r:r + rsub, :], sx, v_ref[pl.ds(start, n_keys), cols(h)], vx_ref[:, cols(h)])
        o_ref[r:r + rsub, cols(h)] = o.astype(BF16)


def _attn_b(q, k, v, kx, vx, table, layer, batch, seq, ctx_len):
    rows = seq // GRID_W
    assert seq % GRID_W == 0 and rows % NB_ROWS == 0 and rows >= NB_KROWS
    nrb = rows // NB_ROWS
    bq = NB_ROWS * GRID_W
    hps = B_HEADS
    assert B_HEADS % hps == 0
    qspec = pl.BlockSpec((bq, hps * HEAD_DIM), lambda h, b, rb: (b * nrb + rb, h))
    kvspec = pl.BlockSpec((seq, hps * HEAD_DIM), lambda h, b, rb: (b, h))
    xspec = pl.BlockSpec((ctx_len, hps * HEAD_DIM), lambda h, b, rb: (b, h))
    tspec = pl.BlockSpec((None, hps, 3, bq, NB_KROWS * GRID_W), lambda h, b, rb: (layer, h, 0, 0, 0),
                         pipeline_mode=pl.Buffered(1))
    return pl.pallas_call(
        functools.partial(_attn_b_kernel, rows=rows, rsub=256),
        grid=(B_HEADS // hps, batch, nrb),
        in_specs=[qspec, kvspec, kvspec, xspec, xspec, tspec],
        out_specs=qspec,
        out_shape=jax.ShapeDtypeStruct(q.shape, BF16),
        compiler_params=_cparams("parallel", "parallel", "parallel"),
        name="attn_neighbourhood",
    )(q, k, v, kx, vx, table)


def _attn_c_kernel(q_ref, k_ref, v_ref, kx_ref, vx_ref, o_ref, vt_ref, *, chunk):
    ctx_len = kx_ref.shape[0]
    n_lat = k_ref.shape[0] // chunk

    @pl.when(pl.program_id(2) == 0)
    def _():
        vt_ref[:, :ctx_len] = vx_ref[...].T
        vt_ref[:, ctx_len:] = v_ref[...].T

    def keys(c):
        return kx_ref[...] if c == 0 else k_ref[(c - 1) * chunk:c * chunk, :]

    def values_t(c):
        return vt_ref[:, :ctx_len] if c == 0 else vt_ref[:, ctx_len + (c - 1) * chunk:ctx_len + c * chunk]

    def scores(unit):
        g, c = unit
        return _dot_t(keys(c), q_ref[:, g * HEAD_DIM:(g + 1) * HEAD_DIM])

    units = [(g, c) for g in range(C_HEADS // C_KV) for c in range(n_lat + 1)]
    s_next = scores(units[0])
    m = l = acc = None
    for idx, (g, c) in enumerate(units):
        s = s_next
        if idx + 1 < len(units):
            s_next = scores(units[idx + 1])
        m_blk = jnp.max(s, axis=0, keepdims=True)
        if c == 0:
            m = m_blk
            p = jnp.exp2(s - m)
            l = jnp.sum(p, axis=0, keepdims=True)
            acc = _dot(values_t(c), p.astype(BF16))
        else:
            m_new = jnp.maximum(m, m_blk)
            alpha = jnp.exp2(m - m_new)
            p = jnp.exp2(s - m_new)
            l = alpha * l + jnp.sum(p, axis=0, keepdims=True)
            acc = alpha * acc + _dot(values_t(c), p.astype(BF16))
            m = m_new
        if c == n_lat:
            o_ref[:, g * HEAD_DIM:(g + 1) * HEAD_DIM] = (acc / l).T.astype(BF16)


def _attn_c(q, k, v, kx, vx, batch, seq, ctx_len):
    bq = 512
    assert seq % bq == 0
    nq = seq // bq
    gw = (C_HEADS // C_KV) * HEAD_DIM
    qspec = pl.BlockSpec((bq, gw), lambda b, kv, i: (b * nq + i, kv))
    kvspec = pl.BlockSpec((seq, HEAD_DIM), lambda b, kv, i: (b, kv))
    xspec = pl.BlockSpec((ctx_len, HEAD_DIM), lambda b, kv, i: (b, kv))
    return pl.pallas_call(
        functools.partial(_attn_c_kernel, chunk=1024),
        grid=(batch, C_KV, nq),
        in_specs=[qspec, kvspec, kvspec, xspec, xspec],
        out_specs=qspec,
        out_shape=jax.ShapeDtypeStruct(q.shape, BF16),
        scratch_shapes=[pltpu.VMEM((HEAD_DIM, ctx_len + seq), BF16)],
        compiler_params=_cparams("parallel", "parallel", "arbitrary"),
        name="attn_global",
    )(q, k, v, kx, vx)


def _softmax_pv_dense(s, v, sink=None):
    m = jnp.max(s, axis=-1, keepdims=True)
    if sink is not None:
        m = jnp.maximum(m, sink)
    e = jnp.exp2(s - m)
    den = jnp.sum(e, axis=-1, keepdims=True)
    if sink is not None:
        den = den + jnp.exp2(sink - m)
    return _dot(e.astype(BF16), v) / den


def _attn_ctx_kernel(sink_ref, qa_ref, ka_ref, va_ref, qb_ref, kb_ref, vb_ref, qc_ref, kc_ref, vc_ref,
                     oa_ref, ob_ref, oc_ref):
    def head(ref, h):
        return ref[:, h * HEAD_DIM:(h + 1) * HEAD_DIM]

    def put(ref, h, o):
        ref[:, h * HEAD_DIM:(h + 1) * HEAD_DIM] = o.astype(BF16)

    ga, gc = A_HEADS // A_KV, C_HEADS // C_KV
    for h in range(A_HEADS):
        s = _dot_t(head(qa_ref, h), head(ka_ref, h // ga))
        put(oa_ref, h, _softmax_pv_dense(s, head(va_ref, h // ga), sink_ref[h] * LOG2E))
    for h in range(B_HEADS):
        put(ob_ref, h, _softmax_pv_dense(_dot_t(head(qb_ref, h), head(kb_ref, h)), head(vb_ref, h)))
    for h in range(C_HEADS):
        s = _dot_t(head(qc_ref, h), head(kc_ref, h // gc))
        put(oc_ref, h, _softmax_pv_dense(s, head(vc_ref, h // gc)))


def _attn_ctx(qkv, sink, batch, ctx_len):
    spec = lambda width: pl.BlockSpec((ctx_len, width), lambda b: (b, 0))
    out_w = (A_WIDTH, B_WIDTH, C_WIDTH)
    return pl.pallas_call(
        _attn_ctx_kernel,
        grid=(batch,),
        in_specs=[pl.BlockSpec(memory_space=pltpu.SMEM)] + [spec(w) for w in QKV_SIZES],
        out_specs=[spec(w) for w in out_w],
        out_shape=[jax.ShapeDtypeStruct((batch * ctx_len, w), BF16) for w in out_w],
        compiler_params=_cparams("parallel"),
        name="attn_ctx",
    )(sink, *qkv)


def _gate_merge_kernel(h_ref, oa_ref, ob_ref, oc_ref, wga_ref, wgb_ref, wgc_ref, wb_ref, m_ref, *, rsub, csub):
    bm, bj = m_ref.shape
    branches = ((oa_ref, wga_ref, 0, A_WIDTH), (ob_ref, wgb_ref, A_WIDTH, A_WIDTH + B_WIDTH),
                (oc_ref, wgc_ref, A_WIDTH + B_WIDTH, A_WIDTH + B_WIDTH + C_WIDTH))

    def dots(unit):
        r, c = unit
        h = h_ref[r:r + rsub, :]
        return [(_dot(h, wg_ref[:, c:c + csub]), _dot(o_ref[r:r + rsub, :], wb_ref[lo:hi, c:c + csub]))
                for o_ref, wg_ref, lo, hi in branches]

    units = [(r, c) for c in range(0, bj, csub) for r in range(0, bm, rsub)]
    nxt = dots(units[0])
    for idx, (r, c) in enumerate(units):
        cur = nxt
        if idx + 1 < len(units):
            nxt = dots(units[idx + 1])
        (ga, ba), (gb, bb), (gc, bc) = cur
        m = jax.nn.sigmoid(ga) * ba + jax.nn.sigmoid(gb) * bb + jax.nn.sigmoid(gc) * bc
        m_ref[r:r + rsub, c:c + csub] = m.astype(BF16)


def _gate_merge(st, h, o_a, o_b, o_c, w_in, w_branch, layer):
    d = h.shape[1]
    bm, bj = st.block(1024), 512
    nj = d // bj
    gate0 = QKV_COLS // bj
    row = lambda width: pl.BlockSpec((bm, width), lambda i, j: (i, 0))
    gspec = lambda k: pl.BlockSpec((None, d, bj), lambda i, j: (layer, 0, gate0 + k * nj + j))
    return pl.pallas_call(
        functools.partial(_gate_merge_kernel, rsub=min(bm, 512), csub=256),
        grid=(st.n_rows // bm, nj),
        in_specs=[row(d), row(A_WIDTH), row(B_WIDTH), row(C_WIDTH), gspec(0), gspec(1), gspec(2),
                  pl.BlockSpec((None, w_branch.shape[1], bj), lambda i, j: (layer, 0, j))],
        out_specs=pl.BlockSpec((bm, bj), lambda i, j: (i, j)),
        out_shape=jax.ShapeDtypeStruct(h.shape, BF16),
        compiler_params=_cparams("parallel", "parallel"),
        name="gate_merge",
    )(h, o_a, o_b, o_c, w_in, w_in, w_in, w_branch)


def _resproj_kernel(*refs, last, rsub):
    x_ref, a_ref, w_ref, gate_ref, gn_ref = refs[:5]
    bm = x_ref.shape[0]

    def project(r):
        return _dot(a_ref[r:r + rsub, :], w_ref[...])

    nxt = project(0)
    for r in range(0, bm, rsub):
        y = nxt
        if r + rsub < bm:
            nxt = project(r + rsub)
        xn = x_ref[r:r + rsub, :] + gate_ref[0] * y
        if last:
            refs[5][r:r + rsub, :] = _rms(xn, gn_ref[...])
        else:
            shn_ref, scn_ref, xo_ref, hn_ref = refs[5:]
            xo_ref[r:r + rsub, :] = xn
            hn_ref[r:r + rsub, :] = _modulated_norm(xn, gn_ref, shn_ref, scn_ref)


def _resproj(st, x, a, w, layer, mod, gate_chunk, g_next, mod_next, next_chunks):
    d = x.shape[1]
    k = a.shape[1]
    rsub = 256
    bm = st.block(2 * rsub if k * d * 2 <= RESIDENT_WEIGHT_BYTES_FOR_WIDE_ROWS else rsub)
    last = mod_next is None
    row = lambda width: pl.BlockSpec((bm, width), lambda i: (i, 0))
    in_specs = [row(d), row(k),
                pl.BlockSpec((None, k, d), lambda i: (layer, 0, 0), pipeline_mode=pl.Buffered(1)),
                st.mod_spec(d, gate_chunk, bm), pl.BlockSpec((1, d), lambda i: (0, 0))]
    args = [x, a, w, mod, g_next.reshape(1, d)]
    if last:
        out_specs = row(d)
        out_shape = jax.ShapeDtypeStruct(x.shape, F32)
    else:
        in_specs += [st.mod_spec(d, next_chunks[0], bm), st.mod_spec(d, next_chunks[1], bm)]
        args += [mod_next, mod_next]
        out_specs = [row(d), row(d)]
        out_shape = [jax.ShapeDtypeStruct(x.shape, F32), jax.ShapeDtypeStruct(x.shape, BF16)]
    return pl.pallas_call(
        functools.partial(_resproj_kernel, last=last, rsub=min(rsub, bm)),
        grid=(st.n_rows // bm,),
        in_specs=in_specs,
        out_specs=out_specs,
        out_shape=out_shape,
        compiler_params=_cparams("parallel"),
        name="resproj",
    )(*args)


HALO = 16


def _convglu_up_kernel(h_ref, hp_ref, hn_ref, wa_ref, wu_ref, cw_ref, cb_ref, g_ref, hext_ref, *, seq, rsub, csub):
    i, j = pl.program_id(0), pl.program_id(1)
    bm, bn = g_ref.shape

    @pl.when(j == 0)
    def _():
        pos = (i * bm) % seq
        zeros = jnp.zeros_like(hp_ref)
        hext_ref[:HALO, :] = jnp.where(pos == 0, zeros, hp_ref[...])
        hext_ref[HALO:HALO + bm, :] = h_ref[...]
        hext_ref[HALO + bm:, :] = jnp.where(pos + bm == seq, zeros, hn_ref[...])

    n_ext = rsub + 2 * HALO
    whole_seqs = bm > seq
    assert rsub == seq or not whole_seqs
    if whole_seqs:
        row = lax.broadcasted_iota(jnp.int32, (n_ext, 1), 0)
        inner = (row >= HALO) & (row < HALO + rsub)

    def gate_dot(unit):
        r, c = unit
        a_ext = _dot(hext_ref[r:r + n_ext, :], wa_ref[:, c:c + csub])
        return jnp.where(inner, a_ext, 0.0) if whole_seqs else a_ext

    units = [(r, c) for c in range(0, bn, csub) for r in range(0, bm, rsub)]
    nxt = gate_dot(units[0])
    for idx, (r, c) in enumerate(units):
        a_ext = nxt
        if idx + 1 < len(units):
            nxt = gate_dot(units[idx + 1])
        a = (pltpu.roll(a_ext, 1, 0) * cw_ref[0:1, c:c + csub] + a_ext * cw_ref[1:2, c:c + csub]
             + pltpu.roll(a_ext, n_ext - 1, 0) * cw_ref[2:3, c:c + csub])[HALO:HALO + rsub] + cb_ref[:, c:c + csub]
        act = a * jax.nn.sigmoid(a)
        u = _dot(h_ref[r:r + rsub, :], wu_ref[:, c:c + csub])
        g_ref[r:r + rsub, c:c + csub] = (act * u).astype(BF16)


def _convglu_up(st, h2, w_up, cw, cb, layer):
    d = h2.shape[1]
    ff = w_up.shape[2] // 2
    bm, bn = st.block(1024), 512
    rsub, csub = min(bm, 512, st.seq), 256
    assert ff % bn == 0 and bm % HALO == 0
    nj = ff // bn
    per_blk = bm // HALO
    n_halo = st.n_rows // HALO
    return pl.pallas_call(
        functools.partial(_convglu_up_kernel, seq=st.seq, rsub=rsub, csub=csub),
        grid=(st.n_rows // bm, nj),
        in_specs=[pl.BlockSpec((bm, d), lambda i, j: (i, 0)),
                  pl.BlockSpec((HALO, d), lambda i, j: (jnp.maximum(i * per_blk - 1, 0), 0)),
                  pl.BlockSpec((HALO, d), lambda i, j: (jnp.minimum((i + 1) * per_blk, n_halo - 1), 0)),
                  pl.BlockSpec((None, d, bn), lambda i, j: (layer, 0, j)),
                  pl.BlockSpec((None, d, bn), lambda i, j: (layer, 0, nj + j)),
                  pl.BlockSpec((None, CONV_W, bn), lambda i, j: (layer, 0, j)),
                  pl.BlockSpec((None, 1, bn), lambda i, j: (layer, 0, j))],
        out_specs=pl.BlockSpec((bm, bn), lambda i, j: (i, j)),
        out_shape=jax.ShapeDtypeStruct((st.n_rows, ff), BF16),
        scratch_shapes=[pltpu.VMEM((bm + 2 * HALO, d), BF16)],
        compiler_params=_cparams("parallel", "arbitrary"),
        name="convglu_up",
    )(h2, h2, h2, w_up, w_up, cw, cb.reshape(cb.shape[0], 1, ff))


def _rope_tables(seq):
    t = jnp.arange(seq)
    row = (t // GRID_W).astype(F32)
    col = (t % GRID_W).astype(F32)
    quarter = HEAD_DIM // 4
    inv = ROPE_THETA ** (-jnp.arange(quarter, dtype=F32) / quarter)
    ang_r = row[:, None] * inv[None, :]
    ang_c = col[:, None] * inv[None, :]
    cr, sr, cc, sc = jnp.cos(ang_r), jnp.sin(ang_r), jnp.cos(ang_c), jnp.sin(ang_c)
    return (jnp.concatenate([cr, cr, cc, cc], axis=-1), jnp.concatenate([-sr, sr, -sc, sc], axis=-1))


def kernel(x, c, ctx, c_ctx, w_ada, b_ada, norm1, w_in, sink_a, rpb_b, qnorm_c, knorm_c, w_branch, w_out,
           norm2, w_up, conv_w, conv_b, w_down, final_norm):
    batch, seq, d = x.shape
    ctx_len = ctx.shape[1]
    depth = w_ada.shape[0]
    assert batch + 1 <= MOD_ROWS

    lat = _Stream(batch * seq, seq, lambda r: r // seq, shared_mod=False)
    cst = _Stream(batch * ctx_len, ctx_len, lambda r: batch, shared_mod=True)

    s = jnp.concatenate([c, c_ctx[None, :], jnp.zeros((MOD_ROWS - batch - 1, d), F32)], axis=0)
    mods = _ada(s, w_ada, b_ada).reshape(depth, MOD_ROWS, 1, 6 * d)
    table = _bias_table(rpb_b)
    rope_tabs = _rope_tables(seq)
    w_in, w_branch, w_out, w_up, w_down = (w.astype(BF16) for w in (w_in, w_branch, w_out, w_up, w_down))

    def mixer_tail(st, xs, hs, o_a, o_b, o_c, l):
        m = _gate_merge(st, hs, o_a, o_b, o_c, w_in, w_branch, l)
        xs, h2 = _resproj(st, xs, m, w_out, l, mods[l], 2, norm2[l], mods[l], (3, 4))
        g = _convglu_up(st, h2, w_up, conv_w, conv_b, l)
        if l == depth - 1:
            return _resproj(st, xs, g, w_down, l, mods[l], 5, final_norm, None, None)
        return _resproj(st, xs, g, w_down, l, mods[l], 5, norm1[l + 1], mods[l + 1], (0, 1))

    xl = x.reshape(batch * seq, d)
    xc = ctx.reshape(batch * ctx_len, d)
    h = hc = out = None
    for l in range(depth):
        if l == 0:
            *qkv, h = _inproj(lat, None, w_in, l, qnorm_c[l], knorm_c[l], rope_tabs, (xl, norm1[l], mods[l]))
            *qkv_x, hc = _inproj(cst, None, w_in, l, qnorm_c[l], knorm_c[l], None, (xc, norm1[l], mods[l]))
        else:
            qkv = _inproj(lat, h, w_in, l, qnorm_c[l], knorm_c[l], rope_tabs)
            qkv_x = _inproj(cst, hc, w_in, l, qnorm_c[l], knorm_c[l], None)
        qa, ka, va, qb, kb, vb, qc, kc, vc = qkv
        _, ka_x, va_x, _, kb_x, vb_x, _, kc_x, vc_x = qkv_x
        o_a = _attn_a(qa, ka, va, ka_x, va_x, sink_a[l], batch, seq, ctx_len)
        o_b = _attn_b(qb, kb, vb, kb_x, vb_x, table, l, batch, seq, ctx_len)
        o_c = _attn_c(qc, kc, vc, kc_x, vc_x, batch, seq, ctx_len)
        if l == depth - 1:
            out = mixer_tail(lat, xl, h, o_a, o_b, o_c, l)
        else:
            xl, h = mixer_tail(lat, xl, h, o_a, o_b, o_c, l)
            oa_x, ob_x, oc_x = _attn_ctx(qkv_x, sink_a[l], batch, ctx_len)
            xc, hc = mixer_tail(cst, xc, hc, oa_x, ob_x, oc_x, l)
    return out.reshape(batch, seq, d)
```

```python
import functools

import jax
import jax.numpy as jnp
from jax import lax
from jax.experimental import pallas as pl
from jax.experimental.pallas import tpu as pltpu

F32 = jnp.float32
BF16 = jnp.bfloat16

GRID_W = 64
HEAD_DIM = 128
A_HEADS, A_KV = 6, 2
B_HEADS = 4
C_HEADS, C_KV = 6, 2
WINDOW = 128
NA_H, NA_W = 8, 16
ROPE_THETA = 10000.0
CONV_W = 3
N_BRANCH = 3
EPS = 1e-6
NEG = -1e30
LOG2E = 1.4426950408889634
QSCALE = HEAD_DIM ** -0.5 * LOG2E
A_WIDTH = A_HEADS * HEAD_DIM
B_WIDTH = B_HEADS * HEAD_DIM
C_WIDTH = C_HEADS * HEAD_DIM
QKV_SIZES = (A_HEADS * HEAD_DIM, A_KV * HEAD_DIM, A_KV * HEAD_DIM,
             B_HEADS * HEAD_DIM, B_HEADS * HEAD_DIM, B_HEADS * HEAD_DIM,
             C_HEADS * HEAD_DIM, C_KV * HEAD_DIM, C_KV * HEAD_DIM)
QKV_COLS = sum(QKV_SIZES)
QKV_KIND = ((True, False, True), (False, False, True), (False, False, False),
            (True, False, False), (False, False, False), (False, False, False),
            (True, True, True), (False, True, True), (False, False, False))
MOD_ROWS = 8
NB_ROWS = 8
NB_KROWS = 16
V7X_VMEM_BYTES = 64 * 1024 * 1024
VMEM_LIMIT = V7X_VMEM_BYTES - 8 * 1024 * 1024
RESIDENT_WEIGHT_BYTES_FOR_WIDE_ROWS = V7X_VMEM_BYTES // 4


def _cparams(*sem):
    return pltpu.CompilerParams(dimension_semantics=sem, vmem_limit_bytes=VMEM_LIMIT)


def _dot(a, b):
    return jnp.dot(a, b, preferred_element_type=F32)


def _dot_t(a, b):
    return lax.dot_general(a, b, (((1,), (1,)), ((), ())), preferred_element_type=F32)


def _rms(x, g):
    return x * lax.rsqrt(jnp.mean(x * x, axis=-1, keepdims=True) + EPS) * g


def _ada_kernel(s_ref, w_ref, b_ref, o_ref):
    s = s_ref[...]
    s = s * jax.nn.sigmoid(s)
    o_ref[0] = _dot(s.astype(BF16), w_ref[0].astype(BF16)) + b_ref[0]


def _ada(s, w_ada, b_ada):
    depth, d, n = w_ada.shape
    bn = 1024
    return pl.pallas_call(
        _ada_kernel,
        grid=(depth, n // bn),
        in_specs=[pl.BlockSpec((MOD_ROWS, d), lambda l, j: (0, 0)),
                  pl.BlockSpec((1, d, bn), lambda l, j: (l, 0, j)),
                  pl.BlockSpec((1, 1, bn), lambda l, j: (l, 0, j))],
        out_specs=pl.BlockSpec((1, MOD_ROWS, bn), lambda l, j: (l, 0, j)),
        out_shape=jax.ShapeDtypeStruct((depth, MOD_ROWS, n), F32),
        compiler_params=_cparams("parallel", "parallel"),
        name="ada",
    )(s, w_ada, b_ada.reshape(depth, 1, n))


def _bias_table_kernel(rpb_ref, o_ref):
    l, h = pl.program_id(0), pl.program_id(1)
    n_dr, n_dc = 2 * NA_H - 1, 2 * NA_W - 1
    base = (l * B_HEADS + h) * n_dr * n_dc
    qc = lax.broadcasted_iota(jnp.int32, (GRID_W, GRID_W), 0)
    kc = lax.broadcasted_iota(jnp.int32, (GRID_W, GRID_W), 1)
    cs = jnp.clip(qc - NA_W // 2, 0, GRID_W - NA_W)
    col_valid = (kc >= cs) & (kc < cs + NA_W)
    dc = jnp.clip(kc - qc + NA_W - 1, 0, 2 * NA_W - 2)
    neg = jnp.full((GRID_W, GRID_W), NEG, F32)
    toe = []
    for d in range(n_dr):
        t = jnp.zeros((GRID_W, GRID_W), F32)
        for e in range(n_dc):
            t = jnp.where(dc == e, rpb_ref[base + d * n_dc + e] * LOG2E, t)
        toe.append(jnp.where(col_valid, t, neg))
    half = NA_H // 2
    for typ in range(3):
        for a in range(NB_ROWS):
            if typ == 0:
                rs_rel, dr0 = max(a - half, 0), NA_H - 1 - a
            elif typ == 1:
                rs_rel, dr0 = a, NA_H - 1 - half - a
            else:
                rs_rel, dr0 = half + min(a, half), NA_H - 1 - NB_ROWS - a
            for jj in range(NB_KROWS):
                valid = rs_rel <= jj < rs_rel + NA_H
                o_ref[0, 0, typ, a * GRID_W:(a + 1) * GRID_W, jj * GRID_W:(jj + 1) * GRID_W] = (
                    toe[jj + dr0] if valid else neg)


def _bias_table(rpb):
    depth = rpb.shape[0]
    shape = (depth, B_HEADS, 3, NB_ROWS * GRID_W, NB_KROWS * GRID_W)
    return pl.pallas_call(
        _bias_table_kernel,
        grid=(depth, B_HEADS),
        in_specs=[pl.BlockSpec(memory_space=pltpu.SMEM)],
        out_specs=pl.BlockSpec((1, 1) + shape[2:], lambda l, h: (l, h, 0, 0, 0)),
        out_shape=jax.ShapeDtypeStruct(shape, F32),
        compiler_params=_cparams("parallel", "parallel"),
        name="bias_table",
    )(rpb.reshape(-1))


class _Stream:
    def __init__(self, n_rows, seq, mod_row, shared_mod):
        assert n_rows % seq == 0
        self.n_rows, self.seq, self.mod_row, self.shared_mod = n_rows, seq, mod_row, shared_mod

    def block(self, preferred):
        bm = min(preferred, self.n_rows if self.shared_mod else self.seq)
        assert self.seq % bm == 0 or (bm % self.seq == 0 and self.n_rows % bm == 0)
        return bm

    def mod_spec(self, d, chunk, bm):
        return pl.BlockSpec((1, 1, d), lambda i, *_: (self.mod_row(i * bm), 0, chunk))


def _modulated_norm(x, g_ref, sh_ref, sc_ref):
    return (_rms(x, g_ref[...]) * (1 + sc_ref[0]) + sh_ref[0]).astype(BF16)


def _swap_half_pairs(x):
    lane = lax.broadcasted_iota(jnp.int32, (1, HEAD_DIM), 1)
    first = (lane % (HEAD_DIM // 2)) < HEAD_DIM // 4
    return jnp.where(first, pltpu.roll(x, HEAD_DIM - HEAD_DIM // 4, 1), pltpu.roll(x, HEAD_DIM // 4, 1))


def _inproj_kernel(*refs, rope, prenorm):
    refs = list(refs)
    if prenorm:
        x_ref, g_ref, sh_ref, sc_ref = refs[:4]
        del refs[:4]
        h_ref = refs.pop()
        h_ref[...] = _modulated_norm(x_ref[...], g_ref, sh_ref, sc_ref)
    else:
        h_ref = refs.pop(0)
    w_ref, qn_ref, kn_ref = refs[:3]
    del refs[:3]
    if rope:
        cos_ref, sin_ref = refs[:2]
        del refs[:2]
    out_refs = refs

    chunks, col = [], 0
    for kind, width, o_ref in zip(QKV_KIND, QKV_SIZES, out_refs):
        chunks += [(kind, o_ref, c, col + c) for c in range(0, width, 2 * HEAD_DIM)]
        col += width

    chunks.sort(key=lambda chunk: -(2 * chunk[0][1] + (chunk[0][2] and rope) + chunk[0][0]))

    def project(chunk):
        return _dot(h_ref[...], w_ref[:, chunk[3]:chunk[3] + 2 * HEAD_DIM])

    nxt = project(chunks[0])
    for idx, ((is_q, has_norm, has_rope), o_ref, c, _) in enumerate(chunks):
        y = nxt
        if idx + 1 < len(chunks):
            nxt = project(chunks[idx + 1])
        for s in range(2):
            yh = y[:, s * HEAD_DIM:(s + 1) * HEAD_DIM]
            if has_norm:
                yh = _rms(yh, (qn_ref if is_q else kn_ref)[...])
            if has_rope and rope:
                yh = yh * cos_ref[...] + _swap_half_pairs(yh) * sin_ref[...]
            if is_q:
                yh = yh * QSCALE
            o_ref[:, c + s * HEAD_DIM:c + (s + 1) * HEAD_DIM] = yh.astype(BF16)


def _inproj(st, h, w_in, layer, qn, kn, rope_tabs, prenorm=None):
    bm = st.block(512)
    rope = rope_tabs is not None
    row = lambda width: pl.BlockSpec((bm, width), lambda i: (i, 0))
    vec = pl.BlockSpec((1, HEAD_DIM), lambda i: (0, 0))
    out_widths = list(QKV_SIZES)
    if prenorm is not None:
        x, g, mod = prenorm
        d = x.shape[1]
        in_specs = [row(d), pl.BlockSpec((1, d), lambda i: (0, 0)), st.mod_spec(d, 0, bm), st.mod_spec(d, 1, bm)]
        args = [x, g.reshape(1, d), mod, mod]
        out_widths.append(d)
    else:
        d = h.shape[1]
        in_specs, args = [row(d)], [h]
    in_specs += [pl.BlockSpec((None, d, QKV_COLS), lambda i: (layer, 0, 0), pipeline_mode=pl.Buffered(1)), vec, vec]
    args += [w_in, qn.reshape(1, HEAD_DIM), kn.reshape(1, HEAD_DIM)]
    if rope:
        per_seq = st.seq // bm
        tab = pl.BlockSpec((bm, HEAD_DIM), lambda i: (i % per_seq, 0))
        in_specs += [tab, tab]
        args += list(rope_tabs)
    return pl.pallas_call(
        functools.partial(_inproj_kernel, rope=rope, prenorm=prenorm is not None),
        grid=(st.n_rows // bm,),
        in_specs=in_specs,
        out_specs=[row(width) for width in out_widths],
        out_shape=[jax.ShapeDtypeStruct((st.n_rows, width), BF16) for width in out_widths],
        compiler_params=_cparams("parallel"),
        name="inproj",
    )(*args)


def _softmax_pv(s, sx, v, vx, sink=None):
    m = jnp.maximum(jnp.max(s, axis=-1, keepdims=True), jnp.max(sx, axis=-1, keepdims=True))
    if sink is not None:
        m = jnp.maximum(m, sink)
    e = jnp.exp2(s - m)
    ex = jnp.exp2(sx - m)
    den = jnp.sum(e, axis=-1, keepdims=True) + jnp.sum(ex, axis=-1, keepdims=True)
    if sink is not None:
        den = den + jnp.exp2(sink - m)
    o = _dot(e.astype(BF16), v) + _dot(ex.astype(BF16), vx)
    return o / den


def _attn_a_kernel(sink_ref, q_ref, k_ref, v_ref, kx_ref, vx_ref, o_ref, *, rsub, seq):
    i = pl.program_id(1)
    bq = q_ref.shape[0]
    wlen = rsub + 2 * WINDOW
    group = A_HEADS // A_KV

    def window(r):
        q0 = i * bq + r
        start = pl.multiple_of(jnp.clip(q0 - WINDOW, 0, seq - wlen), WINDOW)
        qpos = q0 + lax.broadcasted_iota(jnp.int32, (rsub, 1), 0)
        kpos = start + lax.broadcasted_iota(jnp.int32, (1, wlen), 1)
        return start, jnp.abs(kpos - qpos) <= WINDOW

    windows = {r: window(r) for r in range(0, bq, rsub)}

    def kv_cols(h):
        kv = h // group
        return slice(kv * HEAD_DIM, (kv + 1) * HEAD_DIM)

    def scores(unit):
        r, h = unit
        q = q_ref[r:r + rsub, h * HEAD_DIM:(h + 1) * HEAD_DIM]
        return _dot_t(q, k_ref[pl.ds(windows[r][0], wlen), kv_cols(h)]), _dot_t(q, kx_ref[:, kv_cols(h)])

    units = [(r, h) for r in range(0, bq, rsub) for h in range(A_HEADS)]
    nxt = scores(units[0])
    for idx, (r, h) in enumerate(units):
        s, sx = nxt
        if idx + 1 < len(units):
            nxt = scores(units[idx + 1])
        start, valid = windows[r]
        o = _softmax_pv(jnp.where(valid, s, NEG), sx, v_ref[pl.ds(start, wlen), kv_cols(h)], vx_ref[:, kv_cols(h)],
                        sink_ref[h] * LOG2E)
        o_ref[r:r + rsub, h * HEAD_DIM:(h + 1) * HEAD_DIM] = o.astype(BF16)


def _attn_a(q, k, v, kx, vx, sink, batch, seq, ctx_len):
    bq, rsub = 512, 256
    assert seq % bq == 0 and seq >= rsub + 2 * WINDOW
    nq = seq // bq
    qspec = pl.BlockSpec((bq, A_WIDTH), lambda b, i: (b * nq + i, 0))
    kvspec = pl.BlockSpec((seq, A_KV * HEAD_DIM), lambda b, i: (b, 0))
    xspec = pl.BlockSpec((ctx_len, A_KV * HEAD_DIM), lambda b, i: (b, 0))
    return pl.pallas_call(
        functools.partial(_attn_a_kernel, rsub=rsub, seq=seq),
        grid=(batch, nq),
        in_specs=[pl.BlockSpec(memory_space=pltpu.SMEM), qspec, kvspec, kvspec, xspec, xspec],
        out_specs=qspec,
        out_shape=jax.ShapeDtypeStruct(q.shape, BF16),
        compiler_params=_cparams("parallel", "parallel"),
        name="attn_window",
    )(sink, q, k, v, kx, vx)


def _attn_b_kernel(q_ref, k_ref, v_ref, kx_ref, vx_ref, tab_ref, o_ref, *, rows, rsub):
    rb = pl.program_id(2)
    n_rb = pl.num_programs(2)
    typ = jnp.where(rb == 0, 0, jnp.where(rb == n_rb - 1, 2, 1))
    ks = jnp.clip(rb * NB_ROWS - NA_H // 2, 0, rows - NB_KROWS)
    start = pl.multiple_of(ks * GRID_W, GRID_W)
    n_keys = NB_KROWS * GRID_W
    bq = q_ref.shape[0]
    heads = q_ref.shape[1] // HEAD_DIM

    def cols(h):
        return slice(h * HEAD_DIM, (h + 1) * HEAD_DIM)

    def scores(unit):
        h, r = unit
        q = q_ref[r:r + rsub, cols(h)]
        return _dot_t(q, k_ref[pl.ds(start, n_keys), cols(h)]), _dot_t(q, kx_ref[:, cols(h)])

    units = [(h, r) for h in range(heads) for r in range(0, bq, rsub)]
    nxt = scores(units[0])
    for idx, (h, r) in enumerate(units):
        s, sx = nxt
        if idx + 1 < len(units):
            nxt = scores(units[idx + 1])
        o = _softmax_pv(s + tab_ref[h, typ, r:r + rsub, :], sx, v_ref[pl.ds(start, n_keys), cols(h)], vx_ref[:, cols(h)])
        o_ref[r:r + rsub, cols(h)] = o.astype(BF16)


def _attn_b(q, k, v, kx, vx, table, layer, batch, seq, ctx_len):
    rows = seq // GRID_W
    assert seq % GRID_W == 0 and rows % NB_ROWS == 0 and rows >= NB_KROWS
    nrb = rows // NB_ROWS
    bq = NB_ROWS * GRID_W
    hps = B_HEADS
    assert B_HEADS % hps == 0
    qspec = pl.BlockSpec((bq, hps * HEAD_DIM), lambda h, b, rb: (b * nrb + rb, h))
    kvspec = pl.BlockSpec((seq, hps * HEAD_DIM), lambda h, b, rb: (b, h))
    xspec = pl.BlockSpec((ctx_len, hps * HEAD_DIM), lambda h, b, rb: (b, h))
    tspec = pl.BlockSpec((None, hps, 3, bq, NB_KROWS * GRID_W), lambda h, b, rb: (layer, h, 0, 0, 0),
                         pipeline_mode=pl.Buffered(1))
    return pl.pallas_call(
        functools.partial(_attn_b_kernel, rows=rows, rsub=256),
        grid=(B_HEADS // hps, batch, nrb),
        in_specs=[qspec, kvspec, kvspec, xspec, xspec, tspec],
        out_specs=qspec,
        out_shape=jax.ShapeDtypeStruct(q.shape, BF16),
        compiler_params=_cparams("parallel", "parallel", "parallel"),
        name="attn_neighbourhood",
    )(q, k, v, kx, vx, table)


def _attn_c_kernel(q_ref, k_ref, v_ref, kx_ref, vx_ref, o_ref, vt_ref, *, chunk):
    ctx_len = kx_ref.shape[0]
    n_lat = k_ref.shape[0] // chunk

    @pl.when(pl.program_id(2) == 0)
    def _():
        vt_ref[:, :ctx_len] = vx_ref[...].T
        vt_ref[:, ctx_len:] = v_ref[...].T

    def keys(c):
        return kx_ref[...] if c == 0 else k_ref[(c - 1) * chunk:c * chunk, :]

    def values_t(c):
        return vt_ref[:, :ctx_len] if c == 0 else vt_ref[:, ctx_len + (c - 1) * chunk:ctx_len + c * chunk]

    def scores(unit):
        g, c = unit
        return _dot_t(keys(c), q_ref[:, g * HEAD_DIM:(g + 1) * HEAD_DIM])

    units = [(g, c) for g in range(C_HEADS // C_KV) for c in range(n_lat + 1)]
    s_next = scores(units[0])
    m = l = acc = None
    for idx, (g, c) in enumerate(units):
        s = s_next
        if idx + 1 < len(units):
            s_next = scores(units[idx + 1])
        m_blk = jnp.max(s, axis=0, keepdims=True)
        if c == 0:
            m = m_blk
            p = jnp.exp2(s - m)
            l = jnp.sum(p, axis=0, keepdims=True)
            acc = _dot(values_t(c), p.astype(BF16))
        else:
            m_new = jnp.maximum(m, m_blk)
            alpha = jnp.exp2(m - m_new)
            p = jnp.exp2(s - m_new)
            l = alpha * l + jnp.sum(p, axis=0, keepdims=True)
            acc = alpha * acc + _dot(values_t(c), p.astype(BF16))
            m = m_new
        if c == n_lat:
            o_ref[:, g * HEAD_DIM:(g + 1) * HEAD_DIM] = (acc / l).T.astype(BF16)


def _attn_c(q, k, v, kx, vx, batch, seq, ctx_len):
    bq = 512
    assert seq % bq == 0
    nq = seq // bq
    gw = (C_HEADS // C_KV) * HEAD_DIM
    qspec = pl.BlockSpec((bq, gw), lambda b, kv, i: (b * nq + i, kv))
    kvspec = pl.BlockSpec((seq, HEAD_DIM), lambda b, kv, i: (b, kv))
    xspec = pl.BlockSpec((ctx_len, HEAD_DIM), lambda b, kv, i: (b, kv))
    return pl.pallas_call(
        functools.partial(_attn_c_kernel, chunk=1024),
        grid=(batch, C_KV, nq),
        in_specs=[qspec, kvspec, kvspec, xspec, xspec],
        out_specs=qspec,
        out_shape=jax.ShapeDtypeStruct(q.shape, BF16),
        scratch_shapes=[pltpu.VMEM((HEAD_DIM, ctx_len + seq), BF16)],
        compiler_params=_cparams("parallel", "parallel", "arbitrary"),
        name="attn_global",
    )(q, k, v, kx, vx)


def _softmax_pv_dense(s, v, sink=None):
    m = jnp.max(s, axis=-1, keepdims=True)
    if sink is not None:
        m = jnp.maximum(m, sink)
    e = jnp.exp2(s - m)
    den = jnp.sum(e, axis=-1, keepdims=True)
    if sink is not None:
        den = den + jnp.exp2(sink - m)
    return _dot(e.astype(BF16), v) / den


def _attn_ctx_kernel(sink_ref, qa_ref, ka_ref, va_ref, qb_ref, kb_ref, vb_ref, qc_ref, kc_ref, vc_ref,
                     oa_ref, ob_ref, oc_ref):
    def head(ref, h):
        return ref[:, h * HEAD_DIM:(h + 1) * HEAD_DIM]

    def put(ref, h, o):
        ref[:, h * HEAD_DIM:(h + 1) * HEAD_DIM] = o.astype(BF16)

    ga, gc = A_HEADS // A_KV, C_HEADS // C_KV
    for h in range(A_HEADS):
        s = _dot_t(head(qa_ref, h), head(ka_ref, h // ga))
        put(oa_ref, h, _softmax_pv_dense(s, head(va_ref, h // ga), sink_ref[h] * LOG2E))
    for h in range(B_HEADS):
        put(ob_ref, h, _softmax_pv_dense(_dot_t(head(qb_ref, h), head(kb_ref, h)), head(vb_ref, h)))
    for h in range(C_HEADS):
        s = _dot_t(head(qc_ref, h), head(kc_ref, h // gc))
        put(oc_ref, h, _softmax_pv_dense(s, head(vc_ref, h // gc)))


def _attn_ctx(qkv, sink, batch, ctx_len):
    spec = lambda width: pl.BlockSpec((ctx_len, width), lambda b: (b, 0))
    out_w = (A_WIDTH, B_WIDTH, C_WIDTH)
    return pl.pallas_call(
        _attn_ctx_kernel,
        grid=(batch,),
        in_specs=[pl.BlockSpec(memory_space=pltpu.SMEM)] + [spec(w) for w in QKV_SIZES],
        out_specs=[spec(w) for w in out_w],
        out_shape=[jax.ShapeDtypeStruct((batch * ctx_len, w), BF16) for w in out_w],
        compiler_params=_cparams("parallel"),
        name="attn_ctx",
    )(sink, *qkv)


def _gate_merge_kernel(h_ref, oa_ref, ob_ref, oc_ref, wga_ref, wgb_ref, wgc_ref, wb_ref, m_ref, *, rsub, csub):
    bm, bj = m_ref.shape
    branches = ((oa_ref, wga_ref, 0, A_WIDTH), (ob_ref, wgb_ref, A_WIDTH, A_WIDTH + B_WIDTH),
                (oc_ref, wgc_ref, A_WIDTH + B_WIDTH, A_WIDTH + B_WIDTH + C_WIDTH))

    def dots(unit):
        r, c = unit
        h = h_ref[r:r + rsub, :]
        return [(_dot(h, wg_ref[:, c:c + csub]), _dot(o_ref[r:r + rsub, :], wb_ref[lo:hi, c:c + csub]))
                for o_ref, wg_ref, lo, hi in branches]

    units = [(r, c) for c in range(0, bj, csub) for r in range(0, bm, rsub)]
    nxt = dots(units[0])
    for idx, (r, c) in enumerate(units):
        cur = nxt
        if idx + 1 < len(units):
            nxt = dots(units[idx + 1])
        (ga, ba), (gb, bb), (gc, bc) = cur
        m = jax.nn.sigmoid(ga) * ba + jax.nn.sigmoid(gb) * bb + jax.nn.sigmoid(gc) * bc
        m_ref[r:r + rsub, c:c + csub] = m.astype(BF16)


def _gate_merge(st, h, o_a, o_b, o_c, w_in, w_branch, layer):
    d = h.shape[1]
    bm, bj = st.block(1024), 512
    nj = d // bj
    gate0 = QKV_COLS // bj
    row = lambda width: pl.BlockSpec((bm, width), lambda i, j: (i, 0))
    gspec = lambda k: pl.BlockSpec((None, d, bj), lambda i, j: (layer, 0, gate0 + k * nj + j))
    return pl.pallas_call(
        functools.partial(_gate_merge_kernel, rsub=min(bm, 512), csub=256),
        grid=(st.n_rows // bm, nj),
        in_specs=[row(d), row(A_WIDTH), row(B_WIDTH), row(C_WIDTH), gspec(0), gspec(1), gspec(2),
                  pl.BlockSpec((None, w_branch.shape[1], bj), lambda i, j: (layer, 0, j))],
        out_specs=pl.BlockSpec((bm, bj), lambda i, j: (i, j)),
        out_shape=jax.ShapeDtypeStruct(h.shape, BF16),
        compiler_params=_cparams("parallel", "parallel"),
        name="gate_merge",
    )(h, o_a, o_b, o_c, w_in, w_in, w_in, w_branch)


def _resproj_kernel(*refs, last, rsub):
    x_ref, a_ref, w_ref, gate_ref, gn_ref = refs[:5]
    bm = x_ref.shape[0]

    def project(r):
        return _dot(a_ref[r:r + rsub, :], w_ref[...])

    nxt = project(0)
    for r in range(0, bm, rsub):
        y = nxt
        if r + rsub < bm:
            nxt = project(r + rsub)
        xn = x_ref[r:r + rsub, :] + gate_ref[0] * y
        if last:
            refs[5][r:r + rsub, :] = _rms(xn, gn_ref[...])
        else:
            shn_ref, scn_ref, xo_ref, hn_ref = refs[5:]
            xo_ref[r:r + rsub, :] = xn
            hn_ref[r:r + rsub, :] = _modulated_norm(xn, gn_ref, shn_ref, scn_ref)


def _resproj(st, x, a, w, layer, mod, gate_chunk, g_next, mod_next, next_chunks):
    d = x.shape[1]
    k = a.shape[1]
    rsub = 256
    bm = st.block(2 * rsub if k * d * 2 <= RESIDENT_WEIGHT_BYTES_FOR_WIDE_ROWS else rsub)
    last = mod_next is None
    row = lambda width: pl.BlockSpec((bm, width), lambda i: (i, 0))
    in_specs = [row(d), row(k),
                pl.BlockSpec((None, k, d), lambda i: (layer, 0, 0), pipeline_mode=pl.Buffered(1)),
                st.mod_spec(d, gate_chunk, bm), pl.BlockSpec((1, d), lambda i: (0, 0))]
    args = [x, a, w, mod, g_next.reshape(1, d)]
    if last:
        out_specs = row(d)
        out_shape = jax.ShapeDtypeStruct(x.shape, F32)
    else:
        in_specs += [st.mod_spec(d, next_chunks[0], bm), st.mod_spec(d, next_chunks[1], bm)]
        args += [mod_next, mod_next]
        out_specs = [row(d), row(d)]
        out_shape = [jax.ShapeDtypeStruct(x.shape, F32), jax.ShapeDtypeStruct(x.shape, BF16)]
    return pl.pallas_call(
        functools.partial(_resproj_kernel, last=last, rsub=min(rsub, bm)),
        grid=(st.n_rows // bm,),
        in_specs=in_specs,
        out_specs=out_specs,
        out_shape=out_shape,
        compiler_params=_cparams("parallel"),
        name="resproj",
    )(*args)


HALO = 16


def _convglu_up_kernel(h_ref, hp_ref, hn_ref, wa_ref, wu_ref, cw_ref, cb_ref, g_ref, hext_ref, *, seq, rsub, csub):
    i, j = pl.program_id(0), pl.program_id(1)
    bm, bn = g_ref.shape

    @pl.when(j == 0)
    def _():
        pos = (i * bm) % seq
        zeros = jnp.zeros_like(hp_ref)
        hext_ref[:HALO, :] = jnp.where(pos == 0, zeros, hp_ref[...])
        hext_ref[HALO:HALO + bm, :] = h_ref[...]
        hext_ref[HALO + bm:, :] = jnp.where(pos + bm == seq, zeros, hn_ref[...])

    n_ext = rsub + 2 * HALO
    whole_seqs = bm > seq
    assert rsub == seq or not whole_seqs
    if whole_seqs:
        row = lax.broadcasted_iota(jnp.int32, (n_ext, 1), 0)
        inner = (row >= HALO) & (row < HALO + rsub)

    def gate_dot(unit):
        r, c = unit
        a_ext = _dot(hext_ref[r:r + n_ext, :], wa_ref[:, c:c + csub])
        return jnp.where(inner, a_ext, 0.0) if whole_seqs else a_ext

    units = [(r, c) for c in range(0, bn, csub) for r in range(0, bm, rsub)]
    nxt = gate_dot(units[0])
    for idx, (r, c) in enumerate(units):
        a_ext = nxt
        if idx + 1 < len(units):
            nxt = gate_dot(units[idx + 1])
        a = (pltpu.roll(a_ext, 1, 0) * cw_ref[0:1, c:c + csub] + a_ext * cw_ref[1:2, c:c + csub]
             + pltpu.roll(a_ext, n_ext - 1, 0) * cw_ref[2:3, c:c + csub])[HALO:HALO + rsub] + cb_ref[:, c:c + csub]
        act = a * jax.nn.sigmoid(a)
        u = _dot(h_ref[r:r + rsub, :], wu_ref[:, c:c + csub])
        g_ref[r:r + rsub, c:c + csub] = (act * u).astype(BF16)


def _convglu_up(st, h2, w_up, cw, cb, layer):
    d = h2.shape[1]
    ff = w_up.shape[2] // 2
    bm, bn = st.block(1024), 512
    rsub, csub = min(bm, 512, st.seq), 256
    assert ff % bn == 0 and bm % HALO == 0
    nj = ff // bn
    per_blk = bm // HALO
    n_halo = st.n_rows // HALO
    return pl.pallas_call(
        functools.partial(_convglu_up_kernel, seq=st.seq, rsub=rsub, csub=csub),
        grid=(st.n_rows // bm, nj),
        in_specs=[pl.BlockSpec((bm, d), lambda i, j: (i, 0)),
                  pl.BlockSpec((HALO, d), lambda i, j: (jnp.maximum(i * per_blk - 1, 0), 0)),
                  pl.BlockSpec((HALO, d), lambda i, j: (jnp.minimum((i + 1) * per_blk, n_halo - 1), 0)),
                  pl.BlockSpec((None, d, bn), lambda i, j: (layer, 0, j)),
                  pl.BlockSpec((None, d, bn), lambda i, j: (layer, 0, nj + j)),
                  pl.BlockSpec((None, CONV_W, bn), lambda i, j: (layer, 0, j)),
                  pl.BlockSpec((None, 1, bn), lambda i, j: (layer, 0, j))],
        out_specs=pl.BlockSpec((bm, bn), lambda i, j: (i, j)),
        out_shape=jax.ShapeDtypeStruct((st.n_rows, ff), BF16),
        scratch_shapes=[pltpu.VMEM((bm + 2 * HALO, d), BF16)],
        compiler_params=_cparams("parallel", "arbitrary"),
        name="convglu_up",
    )(h2, h2, h2, w_up, w_up, cw, cb.reshape(cb.shape[0], 1, ff))


def _rope_tables(seq):
    t = jnp.arange(seq)
    row = (t // GRID_W).astype(F32)
    col = (t % GRID_W).astype(F32)
    quarter = HEAD_DIM // 4
    inv = ROPE_THETA ** (-jnp.arange(quarter, dtype=F32) / quarter)
    ang_r = row[:, None] * inv[None, :]
    ang_c = col[:, None] * inv[None, :]
    cr, sr, cc, sc = jnp.cos(ang_r), jnp.sin(ang_r), jnp.cos(ang_c), jnp.sin(ang_c)
    return (jnp.concatenate([cr, cr, cc, cc], axis=-1), jnp.concatenate([-sr, sr, -sc, sc], axis=-1))


def kernel(x, c, ctx, c_ctx, w_ada, b_ada, norm1, w_in, sink_a, rpb_b, qnorm_c, knorm_c, w_branch, w_out,
           norm2, w_up, conv_w, conv_b, w_down, final_norm):
    batch, seq, d = x.shape
    ctx_len = ctx.shape[1]
    depth = w_ada.shape[0]
    assert batch + 1 <= MOD_ROWS

    lat = _Stream(batch * seq, seq, lambda r: r // seq, shared_mod=False)
    cst = _Stream(batch * ctx_len, ctx_len, lambda r: batch, shared_mod=True)

    s = jnp.concatenate([c, c_ctx[None, :], jnp.zeros((MOD_ROWS - batch - 1, d), F32)], axis=0)
    mods = _ada(s, w_ada, b_ada).reshape(depth, MOD_ROWS, 1, 6 * d)
    table = _bias_table(rpb_b)
    rope_tabs = _rope_tables(seq)
    w_in, w_branch, w_out, w_up, w_down = (w.astype(BF16) for w in (w_in, w_branch, w_out, w_up, w_down))

    def mixer_tail(st, xs, hs, o_a, o_b, o_c, l):
        m = _gate_merge(st, hs, o_a, o_b, o_c, w_in, w_branch, l)
        xs, h2 = _resproj(st, xs, m, w_out, l, mods[l], 2, norm2[l], mods[l], (3, 4))
        g = _convglu_up(st, h2, w_up, conv_w, conv_b, l)
        if l == depth - 1:
            return _resproj(st, xs, g, w_down, l, mods[l], 5, final_norm, None, None)
        return _resproj(st, xs, g, w_down, l, mods[l], 5, norm1[l + 1], mods[l + 1], (0, 1))

    xl = x.reshape(batch * seq, d)
    xc = ctx.reshape(batch * ctx_len, d)
    h = hc = out = None
    for l in range(depth):
        if l == 0:
            *qkv, h = _inproj(lat, None, w_in, l, qnorm_c[l], knorm_c[l], rope_tabs, (xl, norm1[l], mods[l]))
            *qkv_x, hc = _inproj(cst, None, w_in, l, qnorm_c[l], knorm_c[l], None, (xc, norm1[l], mods[l]))
        else:
            qkv = _inproj(lat, h, w_in, l, qnorm_c[l], knorm_c[l], rope_tabs)
            qkv_x = _inproj(cst, hc, w_in, l, qnorm_c[l], knorm_c[l], None)
        qa, ka, va, qb, kb, vb, qc, kc, vc = qkv
        _, ka_x, va_x, _, kb_x, vb_x, _, kc_x, vc_x = qkv_x
        o_a = _attn_a(qa, ka, va, ka_x, va_x, sink_a[l], batch, seq, ctx_len)
        o_b = _attn_b(qb, kb, vb, kb_x, vb_x, table, l, batch, seq, ctx_len)
        o_c = _attn_c(qc, kc, vc, kc_x, vc_x, batch, seq, ctx_len)
        if l == depth - 1:
            out = mixer_tail(lat, xl, h, o_a, o_b, o_c, l)
        else:
            xl, h = mixer_tail(lat, xl, h, o_a, o_b, o_c, l)
            oa_x, ob_x, oc_x = _attn_ctx(qkv_x, sink_a[l], batch, ctx_len)
            xc, hc = mixer_tail(cst, xc, hc, oa_x, ob_x, oc_x, l)
    return out.reshape(batch, seq, d)
```
